```python
import math, functools
import jax, jax.numpy as jnp
from jax import lax
import numpy as np

D_MODEL = 1024
BATCH = 8
SEQ = 2048
DEPTH = 1
DEC_BATCH = 128
DEC_SEQ = 1
PAST_LEN = 8192
PAGE_SIZE = 128

N_HEADS = 4
HEAD_DIM = 64
QK_DIM = 2 * HEAD_DIM
V_DIM = 2 * HEAD_DIM
QK_WIDTH = N_HEADS * QK_DIM
ATT_WIDTH = N_HEADS * V_DIM
Q_BLOCK = 128
CHUNK = 128
SG_WIDTH = D_MODEL // 2
SG_GROUPS = 4
SG_GROUP_DIM = SG_WIDTH // SG_GROUPS
D_FF = -(-8 * D_MODEL // (3 * 256)) * 256
EPS = 1e-6
SPLIT_POINTS = (
    QK_WIDTH,
    2 * QK_WIDTH,
    2 * QK_WIDTH + ATT_WIDTH,
    2 * QK_WIDTH + ATT_WIDTH + SG_WIDTH,
    2 * QK_WIDTH + ATT_WIDTH + 2 * SG_WIDTH,
    2 * QK_WIDTH + ATT_WIDTH + 2 * SG_WIDTH + D_MODEL,
)
IN_WIDTH = 2 * QK_WIDTH + ATT_WIDTH + 2 * SG_WIDTH + 2 * D_MODEL

kernel_name = 'diff_attn_chunk_gmlp_hybrid_step'


def rmsnorm(x, g):
    xf = x.astype(jnp.float32)
    y = xf * lax.rsqrt(jnp.mean(xf * xf, axis=-1, keepdims=True) + EPS)
    return (y * g.astype(jnp.float32)).astype(x.dtype)


def layernorm(x, g, b):
    xf = x.astype(jnp.float32)
    mu = jnp.mean(xf, axis=-1, keepdims=True)
    xc = xf - mu
    var = jnp.mean(xc * xc, axis=-1, keepdims=True)
    y = xc * lax.rsqrt(var + EPS) * g.astype(jnp.float32) + b.astype(jnp.float32)
    return y.astype(x.dtype)


def diff_weights(s, lam):
    p = jax.nn.softmax(s, axis=-1)
    return p[:, :, 0] - lam * p[:, :, 1]


def diff_attn_prompt(q, k, v, lam):
    B, S = q.shape[:2]
    nqb = S // Q_BLOCK
    qb = q.reshape(B, nqb, Q_BLOCK, N_HEADS, 2, HEAD_DIM).transpose(1, 0, 2, 3, 4, 5)
    kpos = jnp.arange(S)
    scale = HEAD_DIM ** -0.5

    def block(args):
        qi, i = args
        s = jnp.einsum('bqhjd,bkhjd->bhjqk', qi, k).astype(jnp.float32) * scale
        qpos = i * Q_BLOCK + jnp.arange(Q_BLOCK)
        mask = kpos[None, :] <= qpos[:, None]
        s = jnp.where(mask, s, -jnp.inf)
        w = diff_weights(s, lam).astype(v.dtype)
        return jnp.einsum('bhqk,bkhe->bqhe', w, v)

    o = lax.map(block, (qb, jnp.arange(nqb)))
    return o.transpose(1, 0, 2, 3, 4).reshape(B, S, N_HEADS, V_DIM)


def diff_attn_sample(q, k, v, k_pages, v_pages, page_table, lam):
    Bd, T = q.shape[:2]
    k_past = k_pages[page_table].reshape(Bd, -1, N_HEADS, 2, HEAD_DIM)
    v_past = v_pages[page_table].reshape(Bd, -1, N_HEADS, V_DIM)
    P = k_past.shape[1]
    scale = HEAD_DIM ** -0.5
    s_past = jnp.einsum('bqhjd,bkhjd->bhjqk', q, k_past).astype(jnp.float32) * scale
    s_new = jnp.einsum('bqhjd,bkhjd->bhjqk', q, k).astype(jnp.float32) * scale
    causal = jnp.tril(jnp.ones((T, T), dtype=bool))
    s_new = jnp.where(causal, s_new, -jnp.inf)
    s = jnp.concatenate([s_past, s_new], axis=-1)
    w = diff_weights(s, lam).astype(v.dtype)
    o = (jnp.einsum('bhqk,bkhe->bqhe', w[..., :P], v_past)
         + jnp.einsum('bhqk,bkhe->bqhe', w[..., P:], v))
    return o


def spatial_gate(u, vs, w_s, b_s, t_len):
    B, S = u.shape[:2]
    vc = vs.reshape(B, S // t_len, t_len, SG_GROUPS, SG_GROUP_DIM)
    wm = (w_s * jnp.tril(jnp.ones((CHUNK, CHUNK), w_s.dtype)))[:, :t_len, :t_len]
    bias = jnp.transpose(b_s[:, :t_len])[:, :, None]
    s = jnp.einsum('gts,bcsgd->bctgd', wm, vc) + bias
    return u * s.reshape(B, S, SG_WIDTH)


def mixer_inputs(xn, w_in, sg_ln_g, sg_ln_b):
    B, T = xn.shape[:2]
    z = xn @ w_in
    q, k, v, u, vs, ga, gb = jnp.split(z, SPLIT_POINTS, axis=-1)
    q = q.reshape(B, T, N_HEADS, 2, HEAD_DIM)
    k = k.reshape(B, T, N_HEADS, 2, HEAD_DIM)
    v = v.reshape(B, T, N_HEADS, V_DIM)
    u = jax.nn.gelu(u, approximate=False)
    vs = layernorm(jax.nn.gelu(vs, approximate=False), sg_ln_g, sg_ln_b)
    vs = vs.reshape(B, T, SG_GROUPS, SG_GROUP_DIM)
    return q, k, v, u, vs, ga, gb


def trunk_layer(h, attend, t_len, lam_init, w_in, subln_g, sg_ln_g, sg_ln_b, w_spatial, b_spatial,
                w_attn_out, w_sg_out, w_o, norm_mix_g, norm_ffn_g, w_gate, w_up, w_down):
    B, T = h.shape[:2]
    xn = rmsnorm(h, norm_mix_g)
    q, k, v, u, vs, ga, gb = mixer_inputs(xn, w_in, sg_ln_g, sg_ln_b)
    ao = attend(q, k, v)
    ao = (rmsnorm(ao, subln_g) * (1.0 - lam_init)).reshape(B, T, ATT_WIDTH)
    so = spatial_gate(u, vs, w_spatial, b_spatial, t_len)
    m = jax.nn.sigmoid(ga) * (ao @ w_attn_out) + jax.nn.sigmoid(gb) * (so @ w_sg_out)
    h = h + m @ w_o
    hn = rmsnorm(h, norm_ffn_g)
    h = h + (jax.nn.silu(hn @ w_gate) * (hn @ w_up)) @ w_down
    return h, k.reshape(B, T, N_HEADS, QK_DIM), v, vs


def setup_inputs(seed: int = 0) -> dict:
    key = jax.random.key(seed)
    ks = jax.random.split(key, 32)
    f32 = jnp.float32
    n_pages = PAST_LEN // PAGE_SIZE
    n_used = DEC_BATCH * n_pages
    n_pool = n_used + n_used // 4

    def nrm(k, shape, scale):
        return jax.random.normal(k, shape, f32) * scale

    return {
        'x_prompt': nrm(ks[0], (BATCH, SEQ, D_MODEL), 1.0),
        'x_sample': nrm(ks[1], (DEC_BATCH, DEC_SEQ, D_MODEL), 1.0),
        'cache_k': nrm(ks[2], (DEPTH, n_pool, PAGE_SIZE, N_HEADS, QK_DIM), 1.0),
        'cache_v': nrm(ks[3], (DEPTH, n_pool, PAGE_SIZE, N_HEADS, V_DIM), 1.0),
        'page_table': jax.random.permutation(ks[4], n_pool)[:n_used].reshape(DEC_BATCH, n_pages).astype(jnp.int32),
        'w_in': nrm(ks[5], (DEPTH, D_MODEL, IN_WIDTH), D_MODEL ** -0.5),
        'lam_q1': nrm(ks[6], (DEPTH, HEAD_DIM), 0.1),
        'lam_k1': nrm(ks[7], (DEPTH, HEAD_DIM), 0.1),
        'lam_q2': nrm(ks[8], (DEPTH, HEAD_DIM), 0.1),
        'lam_k2': nrm(ks[9], (DEPTH, HEAD_DIM), 0.1),
        'subln_g': 1.0 + nrm(ks[10], (DEPTH, V_DIM), 0.02),
        'sg_ln_g': 1.0 + nrm(ks[11], (DEPTH, SG_WIDTH), 0.02),
        'sg_ln_b': nrm(ks[12], (DEPTH, SG_WIDTH), 0.02),
        'w_spatial': nrm(ks[13], (DEPTH, SG_GROUPS, CHUNK, CHUNK), CHUNK ** -0.5),
        'b_spatial': 1.0 + nrm(ks[14], (DEPTH, SG_GROUPS, CHUNK), 0.02),
        'w_attn_out': nrm(ks[15], (DEPTH, ATT_WIDTH, D_MODEL), ATT_WIDTH ** -0.5),
        'w_sg_out': nrm(ks[16], (DEPTH, SG_WIDTH, D_MODEL), SG_WIDTH ** -0.5),
        'w_o': nrm(ks[17], (DEPTH, D_MODEL, D_MODEL), D_MODEL ** -0.5),
        'norm_mix_g': 1.0 + nrm(ks[18], (DEPTH, D_MODEL), 0.02),
        'norm_ffn_g': 1.0 + nrm(ks[19], (DEPTH, D_MODEL), 0.02),
        'w_gate': nrm(ks[20], (DEPTH, D_MODEL, D_FF), D_MODEL ** -0.5),
        'w_up': nrm(ks[21], (DEPTH, D_MODEL, D_FF), D_MODEL ** -0.5),
        'w_down': nrm(ks[22], (DEPTH, D_FF, D_MODEL), D_FF ** -0.5),
        'norm_final_g': 1.0 + nrm(ks[23], (D_MODEL,), 0.02),
    }


def reference(x_prompt, x_sample, cache_k, cache_v, page_table, w_in, lam_q1, lam_k1, lam_q2, lam_k2,
              subln_g, sg_ln_g, sg_ln_b, w_spatial, b_spatial, w_attn_out, w_sg_out, w_o,
              norm_mix_g, norm_ffn_g, w_gate, w_up, w_down, norm_final_g):
    f32 = jnp.float32
    hp, hs = x_prompt, x_sample
    t_sample = x_sample.shape[1]
    kp_l, vp_l, ks_l, vs_l, sgp_l, sgs_l = [], [], [], [], [], []
    for l in range(DEPTH):
        lam_init = 0.8 - 0.6 * math.exp(-0.3 * l)
        lam = (jnp.exp(jnp.sum(lam_q1[l].astype(f32) * lam_k1[l].astype(f32)))
               - jnp.exp(jnp.sum(lam_q2[l].astype(f32) * lam_k2[l].astype(f32))) + lam_init)
        params = (w_in[l], subln_g[l], sg_ln_g[l], sg_ln_b[l], w_spatial[l], b_spatial[l],
                  w_attn_out[l], w_sg_out[l], w_o[l], norm_mix_g[l], norm_ffn_g[l],
                  w_gate[l], w_up[l], w_down[l])
        attend_p = functools.partial(diff_attn_prompt, lam=lam)
        attend_s = functools.partial(diff_attn_sample, k_pages=cache_k[l], v_pages=cache_v[l],
                                     page_table=page_table, lam=lam)
        hp, kp, vp, sgp = trunk_layer(hp, attend_p, CHUNK, lam_init, *params)
        hs, ksn, vsn, sgs = trunk_layer(hs, attend_s, t_sample, lam_init, *params)
        kp_l.append(kp)
        vp_l.append(vp)
        ks_l.append(ksn)
        vs_l.append(vsn)
        sgp_l.append(sgp[:, -CHUNK:])
        sgs_l.append(sgs)
    y_prompt = rmsnorm(hp, norm_final_g)
    y_sample = rmsnorm(hs, norm_final_g)
    k_prompt = jnp.stack(kp_l)
    v_prompt = jnp.stack(vp_l)
    k_sample = jnp.stack(ks_l)
    v_sample = jnp.stack(vs_l)
    sg_v_prompt = jnp.stack(sgp_l)
    sg_v_sample = jnp.stack(sgs_l)
    return (y_prompt, y_sample, k_prompt, v_prompt, k_sample, v_sample, sg_v_prompt, sg_v_sample)
```

```python
import functools
import math

import jax
import jax.numpy as jnp
from jax import lax
from jax.experimental import pallas as pl
from jax.experimental.pallas import tpu as pltpu

F32 = jnp.float32
BF16 = jnp.bfloat16

N_HEADS = 4
HEAD_DIM = 64
QK_DIM = 2 * HEAD_DIM
V_DIM = 2 * HEAD_DIM
SG_GROUPS = 4
CHUNK = 128
EPS = 1e-6
QK_SCALE = HEAD_DIM ** -0.5
LAM_INIT = 0.8 - 0.6 * math.exp(-0.3 * 0)

LANES = 128
VMEM_LIMIT_BYTES = 56 * 1024 * 1024

ROW_TILE = 256
ATTN_Q_TILE = 256
PAGES_PER_CHUNK = 16
FFN_CHUNKS = 2


def _const_spec(shape):
    zeros = (0,) * len(shape)
    return pl.BlockSpec(shape, lambda *_: zeros, pipeline_mode=pl.Buffered(1))


def _rmsnorm(x, g):
    return x * lax.rsqrt(jnp.mean(x * x, axis=-1, keepdims=True) + EPS) * g


def _gelu(x):
    return 0.5 * x * (1.0 + lax.erf(x * math.sqrt(0.5)))


def _sigmoid(x):
    return 1.0 / (1.0 + jnp.exp(-x))


def _lam(lamp_ref):
    p = lamp_ref[...]
    a = jnp.sum(p[0:1] * p[1:2], axis=-1, keepdims=True)
    b = jnp.sum(p[2:3] * p[3:4], axis=-1, keepdims=True)
    return jnp.exp(a) - jnp.exp(b) + LAM_INIT


def _inproj_kernel(x_ref, gmix_ref, win_ref, lng_ref, lnb_ref, *out_refs, widths, emit_bf16_kv):
    qw, aw, sw, dm = widths
    if emit_bf16_kv:
        q_ref, k_ref, v_ref, kb_ref, vb_ref, u_ref, vs_ref, sga_ref, sgb_ref = out_refs
    else:
        q_ref, k_ref, v_ref, u_ref, vs_ref, sga_ref, sgb_ref = out_refs
    xn = _rmsnorm(x_ref[...], gmix_ref[...]).astype(BF16)

    def proj(lo, width):
        return jnp.dot(xn, win_ref[:, lo:lo + width], preferred_element_type=F32)

    q_ref[...] = (proj(0, qw) * QK_SCALE).astype(BF16)
    k = proj(qw, qw)
    k_ref[...] = k
    v = proj(2 * qw, aw)
    v_ref[...] = v
    if emit_bf16_kv:
        kb_ref[...] = k.astype(BF16)
        vb_ref[...] = v.astype(BF16)
    off = 2 * qw + aw
    u_ref[...] = _gelu(proj(off, sw)).astype(BF16)
    gv = _gelu(proj(off + sw, sw))
    mu = jnp.mean(gv, axis=-1, keepdims=True)
    gc = gv - mu
    var = jnp.mean(gc * gc, axis=-1, keepdims=True)
    vs_ref[...] = gc * lax.rsqrt(var + EPS) * lng_ref[...] + lnb_ref[...]
    off += 2 * sw
    sga_ref[...] = _sigmoid(proj(off, dm)).astype(BF16)
    sgb_ref[...] = _sigmoid(proj(off + dm, dm)).astype(BF16)


def _inproj(x, gmix, win, lng, lnb, *, tile, emit_bf16_kv):
    n, dm = x.shape
    sw = lng.shape[-1]
    qw = N_HEADS * QK_DIM
    aw = N_HEADS * V_DIM
    assert win.shape == (dm, 2 * qw + aw + 2 * sw + 2 * dm)
    assert n % tile == 0
    row = lambda w: pl.BlockSpec((tile, w), lambda i: (i, 0))
    outs = [(qw, BF16), (qw, F32), (aw, F32)]
    if emit_bf16_kv:
        outs += [(qw, BF16), (aw, BF16)]
    outs += [(sw, BF16), (sw, F32), (dm, BF16), (dm, BF16)]
    return pl.pallas_call(
        functools.partial(_inproj_kernel, widths=(qw, aw, sw, dm), emit_bf16_kv=emit_bf16_kv),
        grid=(n // tile,),
        in_specs=[row(dm), _const_spec((1, dm)), _const_spec(win.shape),
                  _const_spec((1, sw)), _const_spec((1, sw))],
        out_specs=[row(w) for w, _ in outs],
        out_shape=[jax.ShapeDtypeStruct((n, w), dt) for w, dt in outs],
        compiler_params=pltpu.CompilerParams(
            dimension_semantics=("arbitrary",), vmem_limit_bytes=VMEM_LIMIT_BYTES),
        name="inproj",
    )(x, gmix, win, lng, lnb)


def _subln(o, lam_free_gain):
    return o * lax.rsqrt(jnp.mean(o * o, axis=-1, keepdims=True) + EPS) * lam_free_gain


def _prompt_attn_kernel(q_ref, k_ref, v_ref, lamp_ref, sg_ref, o_ref, *, tq):
    seq = q_ref.shape[1]
    lam = _lam(lamp_ref)
    gain = sg_ref[...] * (1.0 - LAM_INIT)
    lane = lax.broadcasted_iota(jnp.int32, (tq, QK_DIM), 1)
    first_map = lane < HEAD_DIM
    row = lax.broadcasted_iota(jnp.int32, (2 * tq, tq), 0)
    col = lax.broadcasted_iota(jnp.int32, (2 * tq, tq), 1)
    causal = col <= jnp.where(row >= tq, row - tq, row)
    for i in range(seq // tq):
        kv = (i + 1) * tq
        q = q_ref[0, i * tq:(i + 1) * tq, :]
        zero = jnp.zeros_like(q)
        qq = jnp.concatenate([jnp.where(first_map, q, zero), jnp.where(first_map, zero, q)], axis=0)
        s = lax.dot_general(qq, k_ref[0, :kv, :], (((1,), (1,)), ((), ())),
                            preferred_element_type=F32)
        s_diag = jnp.where(causal, s[:, kv - tq:], -jnp.inf)
        s = s_diag if i == 0 else jnp.concatenate([s[:, :kv - tq], s_diag], axis=1)
        m = jnp.max(s, axis=-1, keepdims=True)
        p = jnp.exp(s - m)
        r = 1.0 / jnp.sum(p, axis=-1, keepdims=True)
        w = p[:tq] * r[:tq] - lam * (p[tq:] * r[tq:])
        o = jnp.dot(w.astype(BF16), v_ref[0, :kv, :], preferred_element_type=F32)
        o_ref[0, i * tq:(i + 1) * tq, :] = _subln(o, gain).astype(o_ref.dtype)


def _prompt_attn(q, k, v, lamp, subln_g):
    b, s, _ = q.shape
    head = pl.BlockSpec((1, s, QK_DIM), lambda bi, hi: (bi, 0, hi))
    return pl.pallas_call(
        functools.partial(_prompt_attn_kernel, tq=ATTN_Q_TILE),
        grid=(b, N_HEADS),
        in_specs=[head, head, head, _const_spec(lamp.shape), _const_spec(subln_g.shape)],
        out_specs=pl.BlockSpec((1, s, V_DIM), lambda bi, hi: (bi, 0, hi)),
        out_shape=jax.ShapeDtypeStruct((b, s, N_HEADS * V_DIM), BF16),
        compiler_params=pltpu.CompilerParams(
            dimension_semantics=("arbitrary", "arbitrary"), vmem_limit_bytes=VMEM_LIMIT_BYTES),
        name="prompt_attn",
    )(q, k, v, lamp, subln_g)


def _decode_attn_kernel(pt_ref, q_ref, kn_ref, vn_ref, lamp_ref, sg_ref, ck_hbm, cv_hbm,
                        o_ref, kbuf, vbuf, sem, *, n_pages, pages_per_chunk):
    n_req = q_ref.shape[0]
    width = q_ref.shape[2]
    n_rows = 2 * N_HEADS
    n_chunks = n_pages // pages_per_chunk
    page = kbuf.shape[1] // pages_per_chunk

    def page_copies(r, c, slot, p):
        idx = pt_ref[r * n_pages + c * pages_per_chunk + p]
        rows = pl.ds(p * page, page)
        return (pltpu.make_async_copy(ck_hbm.at[idx], kbuf.at[slot, rows], sem.at[slot, 0]),
                pltpu.make_async_copy(cv_hbm.at[idx], vbuf.at[slot, rows], sem.at[slot, 1]))

    def start_chunk(r, c, slot):
        for p in range(pages_per_chunk):
            for cp in page_copies(r, c, slot, p):
                cp.start()

    def wait_chunk(r, c, slot):
        for p in range(pages_per_chunk):
            for cp in page_copies(r, c, slot, p):
                cp.wait()

    lam = _lam(lamp_ref)
    gain = sg_ref[...] * (1.0 - LAM_INIT)
    rr = lax.broadcasted_iota(jnp.int32, (n_rows, width), 0)
    ll = lax.broadcasted_iota(jnp.int32, (n_rows, width), 1)
    own = ll // HEAD_DIM == rr

    start_chunk(0, 0, 0)

    def request(r, carry):
        q = q_ref[r]
        qb = jnp.where(own, jnp.broadcast_to(q, (n_rows, width)), 0.0).astype(BF16)
        kn = kn_ref[r].astype(BF16).astype(F32)
        vn = vn_ref[r].astype(BF16).astype(F32)
        m = jnp.sum(qb.astype(F32) * kn, axis=-1, keepdims=True)
        l = jnp.ones_like(m)
        acc = jnp.broadcast_to(vn, (n_rows, width))
        for c in range(n_chunks):
            slot = c % 2
            if c + 1 < n_chunks:
                start_chunk(r, c + 1, 1 - slot)
            else:
                @pl.when(r + 1 < n_req)
                def _():
                    start_chunk(r + 1, 0, 1 - slot)
            wait_chunk(r, c, slot)
            kb = kbuf[slot].astype(BF16)
            s = lax.dot_general(qb, kb, (((1,), (1,)), ((), ())), preferred_element_type=F32)
            m_new = jnp.maximum(m, jnp.max(s, axis=-1, keepdims=True))
            alpha = jnp.exp(m - m_new)
            p = jnp.exp(s - m_new)
            l = alpha * l + jnp.sum(p, axis=-1, keepdims=True)
            acc = alpha * acc + jnp.dot(p.astype(BF16), vbuf[slot].astype(BF16),
                                        preferred_element_type=F32)
            m = m_new
        a = acc * (1.0 / l)
        for h in range(N_HEADS):
            lanes = slice(h * V_DIM, (h + 1) * V_DIM)
            o = a[2 * h:2 * h + 1, lanes] - lam * a[2 * h + 1:2 * h + 2, lanes]
            o_ref[r, :, lanes] = _subln(o, gain)
        return carry

    lax.fori_loop(0, n_req, request, 0)


def _decode_attn(q, k_new, v_new, cache_k, cache_v, page_table, lamp, subln_g):
    n_req, width = q.shape
    as_rows = lambda a: a.astype(F32).reshape(n_req, 1, width)
    n_pages = page_table.shape[1]
    page = cache_k.shape[1]
    assert n_pages % (2 * PAGES_PER_CHUNK) == 0
    vmem = pl.BlockSpec(memory_space=pltpu.VMEM)
    hbm = pl.BlockSpec(memory_space=pl.ANY)
    rows = PAGES_PER_CHUNK * page
    out = pl.pallas_call(
        functools.partial(_decode_attn_kernel, n_pages=n_pages, pages_per_chunk=PAGES_PER_CHUNK),
        grid_spec=pltpu.PrefetchScalarGridSpec(
            num_scalar_prefetch=1,
            grid=(1,),
            in_specs=[vmem, vmem, vmem, vmem, vmem, hbm, hbm],
            out_specs=vmem,
            scratch_shapes=[pltpu.VMEM((2, rows, width), cache_k.dtype),
                            pltpu.VMEM((2, rows, width), cache_v.dtype),
                            pltpu.SemaphoreType.DMA((2, 2))],
        ),
        out_shape=jax.ShapeDtypeStruct((n_req, 1, width), F32),
        compiler_params=pltpu.CompilerParams(
            dimension_semantics=("arbitrary",), vmem_limit_bytes=VMEM_LIMIT_BYTES),
        name="decode_attn",
    )(page_table.reshape(-1), as_rows(q), as_rows(k_new), as_rows(v_new), lamp, subln_g,
      cache_k, cache_v)
    return out.reshape(n_req, width)


def _merge_ffn_kernel(x_ref, ao_ref, u_ref, vs_ref, sga_ref, sgb_ref, ws_ref, bs_ref,
                      wao_ref, wso_ref, wo_ref, gffn_ref, wg_ref, wu_ref, wd_ref, gfin_ref,
                      y_ref, *, chunked):
    tile = x_ref.shape[0]
    sw = u_ref.shape[1]
    gd = sw // SG_GROUPS
    u = u_ref[...].astype(F32)
    if chunked:
        vs = vs_ref[...].astype(BF16)
        n_chunks = tile // CHUNK
        tri = (lax.broadcasted_iota(jnp.int32, (CHUNK, CHUNK), 0)
               >= lax.broadcasted_iota(jnp.int32, (CHUNK, CHUNK), 1))
        groups = []
        for g in range(SG_GROUPS):
            wg = jnp.where(tri, ws_ref[g], 0.0).astype(BF16)
            vg = jnp.concatenate([vs[c * CHUNK:(c + 1) * CHUNK, g * gd:(g + 1) * gd]
                                  for c in range(n_chunks)], axis=1)
            sg = jnp.dot(wg, vg, preferred_element_type=F32) + bs_ref[:, g:g + 1]
            groups.append(jnp.concatenate([sg[:, c * gd:(c + 1) * gd] for c in range(n_chunks)],
                                          axis=0))
        s = jnp.concatenate(groups, axis=1)
    else:
        s = vs_ref[...] * ws_ref[...] + bs_ref[...]
    so = (u * s).astype(BF16)
    attn = jnp.dot(ao_ref[...].astype(BF16), wao_ref[...], preferred_element_type=F32)
    gate = jnp.dot(so, wso_ref[...], preferred_element_type=F32)
    m = sga_ref[...].astype(F32) * attn + sgb_ref[...].astype(F32) * gate
    h = x_ref[...] + jnp.dot(m.astype(BF16), wo_ref[...], preferred_element_type=F32)
    hn = _rmsnorm(h, gffn_ref[...]).astype(BF16)
    d_ff = wg_ref.shape[1]
    step = d_ff // FFN_CHUNKS
    for c in range(FFN_CHUNKS):
        cols = slice(c * step, (c + 1) * step)
        a = jnp.dot(hn, wg_ref[:, cols], preferred_element_type=F32)
        b = jnp.dot(hn, wu_ref[:, cols], preferred_element_type=F32)
        f = (a * _sigmoid(a) * b).astype(BF16)
        h = h + jnp.dot(f, wd_ref[cols, :], preferred_element_type=F32)
    y_ref[...] = _rmsnorm(h, gfin_ref[...])


def _merge_ffn(x, ao, u, vs, sga, sgb, ws, bs, wao, wso, wo, gffn, wg, wu, wd, gfin, *, tile, chunked):
    n, dm = x.shape
    assert n % tile == 0 and wg.shape[1] % (FFN_CHUNKS * LANES) == 0
    row = lambda a: pl.BlockSpec((tile, a.shape[1]), lambda i: (i, 0))
    consts = (ws, bs, wao, wso, wo, gffn, wg, wu, wd, gfin)
    return pl.pallas_call(
        functools.partial(_merge_ffn_kernel, chunked=chunked),
        grid=(n // tile,),
        in_specs=[row(a) for a in (x, ao, u, vs, sga, sgb)] + [_const_spec(a.shape) for a in consts],
        out_specs=pl.BlockSpec((tile, dm), lambda i: (i, 0)),
        out_shape=jax.ShapeDtypeStruct((n, dm), F32),
        compiler_params=pltpu.CompilerParams(
            dimension_semantics=("arbitrary",), vmem_limit_bytes=VMEM_LIMIT_BYTES),
        name="merge_ffn_chunked" if chunked else "merge_ffn_rows",
    )(x, ao, u, vs, sga, sgb, *consts)


def kernel(x_prompt, x_sample, cache_k, cache_v, page_table, w_in, lam_q1, lam_k1, lam_q2, lam_k2,
           subln_g, sg_ln_g, sg_ln_b, w_spatial, b_spatial, w_attn_out, w_sg_out, w_o,
           norm_mix_g, norm_ffn_g, w_gate, w_up, w_down, norm_final_g):
    assert w_in.shape[0] == 1, "single-layer trunk"
    b, s, dm = x_prompt.shape
    bd, t_s, _ = x_sample.shape
    assert t_s == 1 and s % CHUNK == 0
    n_pool, page = cache_k.shape[1], cache_k.shape[2]
    qw, aw = N_HEADS * QK_DIM, N_HEADS * V_DIM
    sw = sg_ln_g.shape[-1]
    gd = sw // SG_GROUPS

    row1 = lambda a: a.reshape(1, -1)
    win = w_in[0].astype(BF16)
    gmix, lng, lnb = row1(norm_mix_g[0]), row1(sg_ln_g[0]), row1(sg_ln_b[0])
    lamp = jnp.stack([lam_q1[0], lam_k1[0], lam_q2[0], lam_k2[0]])
    subg = row1(subln_g[0])
    merge_w = (w_attn_out[0].astype(BF16), w_sg_out[0].astype(BF16), w_o[0].astype(BF16),
               row1(norm_ffn_g[0]), w_gate[0].astype(BF16), w_up[0].astype(BF16),
               w_down[0].astype(BF16), row1(norm_final_g))

    xp = x_prompt.reshape(b * s, dm)
    xs = x_sample.reshape(bd, dm)

    qp, kp, vp, kbp, vbp, up, vsp, sgap, sgbp = _inproj(
        xp, gmix, win, lng, lnb, tile=ROW_TILE, emit_bf16_kv=True)
    qs, ks, vs_, us, vss, sgas, sgbs = _inproj(
        xs, gmix, win, lng, lnb, tile=bd, emit_bf16_kv=False)

    ao_p = _prompt_attn(qp.reshape(b, s, qw), kbp.reshape(b, s, qw), vbp.reshape(b, s, aw),
                        lamp, subg)
    ao_s = _decode_attn(qs, ks, vs_, cache_k[0].reshape(n_pool, page, qw),
                        cache_v[0].reshape(n_pool, page, aw), page_table, lamp, subg)

    yp = _merge_ffn(xp, ao_p.reshape(b * s, aw), up, vsp, sgap, sgbp,
                    w_spatial[0], jnp.transpose(b_spatial[0]), *merge_w,
                    tile=ROW_TILE, chunked=True)
    ys = _merge_ffn(xs, ao_s, us, vss, sgas, sgbs,
                    row1(jnp.repeat(w_spatial[0, :, 0, 0], gd)), row1(jnp.repeat(b_spatial[0, :, 0], gd)),
                    *merge_w, tile=bd, chunked=False)

    y_prompt = yp.reshape(b, s, dm)
    y_sample = ys.reshape(bd, 1, dm)
    k_prompt = kp.reshape(1, b, s, N_HEADS, QK_DIM)
    v_prompt = vp.reshape(1, b, s, N_HEADS, V_DIM)
    k_sample = ks.reshape(1, bd, 1, N_HEADS, QK_DIM)
    v_sample = vs_.reshape(1, bd, 1, N_HEADS, V_DIM)
    sg_v_prompt = vsp.reshape(b, s, SG_GROUPS, gd)[:, s - CHUNK:].reshape(1, b, CHUNK, SG_GROUPS, gd)
    sg_v_sample = vss.reshape(1, bd, 1, SG_GROUPS, gd)
    return (y_prompt, y_sample, k_prompt, v_prompt, k_sample, v_sample, sg_v_prompt, sg_v_sample)
```

```python
import functools
import math

import jax
import jax.numpy as jnp
from jax import lax
from jax.experimental import pallas as pl
from jax.experimental.pallas import tpu as pltpu

F32 = jnp.float32
BF16 = jnp.bfloat16

N_HEADS = 4
HEAD_DIM = 64
QK_DIM = 2 * HEAD_DIM
V_DIM = 2 * HEAD_DIM
SG_GROUPS = 4
CHUNK = 128
EPS = 1e-6
QK_SCALE = HEAD_DIM ** -0.5
LAM_INIT = 0.8 - 0.6 * math.exp(-0.3 * 0)

LANES = 128
VMEM_LIMIT_BYTES = 56 * 1024 * 1024

ROW_TILE = 256
ATTN_Q_TILE = 256
PAGES_PER_CHUNK = 16
FFN_CHUNKS = 2


def _const_spec(shape):
    zeros = (0,) * len(shape)
    return pl.BlockSpec(shape, lambda *_: zeros, pipeline_mode=pl.Buffered(1))


def _rmsnorm(x, g):
    return x * lax.rsqrt(jnp.mean(x * x, axis=-1, keepdims=True) + EPS) * g


def _gelu(x):
    return 0.5 * x * (1.0 + lax.erf(x * math.sqrt(0.5)))


def _sigmoid(x):
    return 1.0 / (1.0 + jnp.exp(-x))


def _lam(lamp_ref):
    p = lamp_ref[...]
    a = jnp.sum(p[0:1] * p[1:2], axis=-1, keepdims=True)
    b = jnp.sum(p[2:3] * p[3:4], axis=-1, keepdims=True)
    return jnp.exp(a) - jnp.exp(b) + LAM_INIT


def _subln(o, gain):
    return o * lax.rsqrt(jnp.mean(o * o, axis=-1, keepdims=True) + EPS) * gain


def _inproj_kernel(x_ref, gmix_ref, win_ref, lng_ref, lnb_ref, *out_refs, widths, emit_bf16_kv):
    qw, aw, sw, dm = widths
    if emit_bf16_kv:
        q_ref, k_ref, v_ref, kb_ref, vb_ref, u_ref, vs_ref, sga_ref, sgb_ref = out_refs
    else:
        q_ref, k_ref, v_ref, u_ref, vs_ref, sga_ref, sgb_ref = out_refs
    tile = x_ref.shape[0]
    xn = _rmsnorm(x_ref[...], gmix_ref[...]).astype(BF16)

    def proj(lo, width):
        return jnp.dot(xn, win_ref[:, lo:lo + width], preferred_element_type=F32)

    def store_by_head(ref, val, head_dim):
        for h in range(N_HEADS):
            ref[pl.ds(h, tile, stride=N_HEADS), :] = val[:, h * head_dim:(h + 1) * head_dim]

    q_ref[...] = (proj(0, qw) * QK_SCALE).astype(BF16)
    k = proj(qw, qw)
    store_by_head(k_ref, k, QK_DIM)
    v = proj(2 * qw, aw)
    store_by_head(v_ref, v, V_DIM)
    if emit_bf16_kv:
        kb_ref[...] = k.astype(BF16)
        vb_ref[...] = v.astype(BF16)
    off = 2 * qw + aw
    u_ref[...] = _gelu(proj(off, sw)).astype(BF16)
    gv = _gelu(proj(off + sw, sw))
    mu = jnp.mean(gv, axis=-1, keepdims=True)
    gc = gv - mu
    var = jnp.mean(gc * gc, axis=-1, keepdims=True)
    vs_ref[...] = gc * lax.rsqrt(var + EPS) * lng_ref[...] + lnb_ref[...]
    off += 2 * sw
    sga_ref[...] = _sigmoid(proj(off, dm)).astype(BF16)
    sgb_ref[...] = _sigmoid(proj(off + dm, dm)).astype(BF16)


def _inproj(x, gmix, win, lng, lnb, *, tile, emit_bf16_kv):
    n, dm = x.shape
    sw = lng.shape[-1]
    qw = N_HEADS * QK_DIM
    aw = N_HEADS * V_DIM
    assert win.shape == (dm, 2 * qw + aw + 2 * sw + 2 * dm)
    assert n % tile == 0
    row = lambda w: pl.BlockSpec((tile, w), lambda i: (i, 0))
    by_head = lambda w: pl.BlockSpec((tile * N_HEADS, w), lambda i: (i, 0))
    outs = [(n, qw, BF16, row(qw)),
            (n * N_HEADS, QK_DIM, F32, by_head(QK_DIM)),
            (n * N_HEADS, V_DIM, F32, by_head(V_DIM))]
    if emit_bf16_kv:
        outs += [(n, qw, BF16, row(qw)), (n, aw, BF16, row(aw))]
    outs += [(n, sw, BF16, row(sw)), (n, sw, F32, row(sw)),
             (n, dm, BF16, row(dm)), (n, dm, BF16, row(dm))]
    return pl.pallas_call(
        functools.partial(_inproj_kernel, widths=(qw, aw, sw, dm), emit_bf16_kv=emit_bf16_kv),
        grid=(n // tile,),
        in_specs=[row(dm), _const_spec((1, dm)), _const_spec(win.shape),
                  _const_spec((1, sw)), _const_spec((1, sw))],
        out_specs=[o[3] for o in outs],
        out_shape=[jax.ShapeDtypeStruct((o[0], o[1]), o[2]) for o in outs],
        compiler_params=pltpu.CompilerParams(
            dimension_semantics=("arbitrary",), vmem_limit_bytes=VMEM_LIMIT_BYTES),
        name="inproj",
    )(x, gmix, win, lng, lnb)


def _prompt_attn_kernel(q_ref, k_ref, v_ref, lamp_ref, sg_ref, o_ref, *, tq):
    seq = q_ref.shape[1]
    lam = _lam(lamp_ref)
    gain = sg_ref[...] * (1.0 - LAM_INIT)
    lane = lax.broadcasted_iota(jnp.int32, (tq, QK_DIM), 1)
    first_map = lane < HEAD_DIM
    row = lax.broadcasted_iota(jnp.int32, (2 * tq, tq), 0)
    col = lax.broadcasted_iota(jnp.int32, (2 * tq, tq), 1)
    causal = col <= jnp.where(row >= tq, row - tq, row)
    for i in range(seq // tq):
        kv = (i + 1) * tq
        q = q_ref[0, i * tq:(i + 1) * tq, :]
        zero = jnp.zeros_like(q)
        qq = jnp.concatenate([jnp.where(first_map, q, zero), jnp.where(first_map, zero, q)], axis=0)
        s = lax.dot_general(qq, k_ref[0, :kv, :], (((1,), (1,)), ((), ())),
                            preferred_element_type=F32)
        s_diag = jnp.where(causal, s[:, kv - tq:], -jnp.inf)
        s = s_diag if i == 0 else jnp.concatenate([s[:, :kv - tq], s_diag], axis=1)
        m = jnp.max(s, axis=-1, keepdims=True)
        p = jnp.exp(s - m)
        r = 1.0 / jnp.sum(p, axis=-1, keepdims=True)
        w = p[:tq] * r[:tq] - lam * (p[tq:] * r[tq:])
        o = jnp.dot(w.astype(BF16), v_ref[0, :kv, :], preferred_element_type=F32)
        o_ref[0, i * tq:(i + 1) * tq, :] = _subln(o, gain).astype(o_ref.dtype)


def _prompt_attn(q, k, v, lamp, subln_g):
    b, s, _ = q.shape
    head = pl.BlockSpec((1, s, QK_DIM), lambda bi, hi: (bi, 0, hi))
    return pl.pallas_call(
        functools.partial(_prompt_attn_kernel, tq=ATTN_Q_TILE),
        grid=(b, N_HEADS),
        in_specs=[head, head, head, _const_spec(lamp.shape), _const_spec(subln_g.shape)],
        out_specs=pl.BlockSpec((1, s, V_DIM), lambda bi, hi: (bi, 0, hi)),
        out_shape=jax.ShapeDtypeStruct((b, s, N_HEADS * V_DIM), BF16),
        compiler_params=pltpu.CompilerParams(
            dimension_semantics=("arbitrary", "arbitrary"), vmem_limit_bytes=VMEM_LIMIT_BYTES),
        name="prompt_attn",
    )(q, k, v, lamp, subln_g)


def _decode_attn_kernel(pt_ref, q_ref, kn_ref, vn_ref, lamp_ref, sg_ref, ck_hbm, cv_hbm,
                        o_ref, kbuf, vbuf, sem, *, n_pages, pages_per_chunk):
    n_req, n_rows, _ = q_ref.shape
    chunk_rows = kbuf.shape[1]
    page_rows = chunk_rows // pages_per_chunk
    n_chunks = n_pages // pages_per_chunk

    def page_copies(r, c, slot, p):
        idx = pt_ref[r * n_pages + c * pages_per_chunk + p]
        src = pl.ds(pl.multiple_of(idx * page_rows, page_rows), page_rows)
        dst = pl.ds(p * page_rows, page_rows)
        return (pltpu.make_async_copy(ck_hbm.at[src], kbuf.at[slot, dst], sem.at[slot, 0]),
                pltpu.make_async_copy(cv_hbm.at[src], vbuf.at[slot, dst], sem.at[slot, 1]))

    def start_chunk(r, c, slot):
        for p in range(pages_per_chunk):
            for cp in page_copies(r, c, slot, p):
                cp.start()

    def wait_chunk(r, c, slot):
        for p in range(pages_per_chunk):
            for cp in page_copies(r, c, slot, p):
                cp.wait()

    lam = _lam(lamp_ref)
    gain = sg_ref[...] * (1.0 - LAM_INIT)
    own_map = (lax.broadcasted_iota(jnp.int32, (n_rows, QK_DIM), 1) // HEAD_DIM
               == lax.broadcasted_iota(jnp.int32, (n_rows, QK_DIM), 0) % 2)
    own_head = (lax.broadcasted_iota(jnp.int32, (n_rows, chunk_rows), 1) % N_HEADS
                == lax.broadcasted_iota(jnp.int32, (n_rows, chunk_rows), 0) // 2)

    start_chunk(0, 0, 0)

    def request(r, carry):
        qb = jnp.where(own_map, q_ref[r], 0.0).astype(BF16)
        kn = kn_ref[r].astype(BF16).astype(F32)
        m = jnp.sum(qb.astype(F32) * kn, axis=-1, keepdims=True)
        l = jnp.ones_like(m)
        acc = vn_ref[r].astype(BF16).astype(F32)
        for c in range(n_chunks):
            slot = c % 2
            if c + 1 < n_chunks:
                start_chunk(r, c + 1, 1 - slot)
            else:
                @pl.when(r + 1 < n_req)
                def _():
                    start_chunk(r + 1, 0, 1 - slot)
            wait_chunk(r, c, slot)
            kb = kbuf[slot].astype(BF16)
            s = lax.dot_general(qb, kb, (((1,), (1,)), ((), ())), preferred_element_type=F32)
            s = jnp.where(own_head, s, -jnp.inf)
            m_new = jnp.maximum(m, jnp.max(s, axis=-1, keepdims=True))
            alpha = jnp.exp(m - m_new)
            p = jnp.exp(s - m_new)
            l = alpha * l + jnp.sum(p, axis=-1, keepdims=True)
            acc = alpha * acc + jnp.dot(p.astype(BF16), vbuf[slot].astype(BF16),
                                        preferred_element_type=F32)
            m = m_new
        a = acc * (1.0 / l)
        for h in range(N_HEADS):
            o = a[2 * h:2 * h + 1] - lam * a[2 * h + 1:2 * h + 2]
            o_ref[r, h:h + 1, :] = _subln(o, gain)
        return carry

    lax.fori_loop(0, n_req, request, 0)


def _decode_attn(q, k_new, v_new, cache_k, cache_v, page_table, lamp, subln_g, *, page_rows):
    n_req = q.shape[0]
    n_pages = page_table.shape[1]
    assert n_pages % (2 * PAGES_PER_CHUNK) == 0
    per_map = lambda a: jnp.repeat(a.astype(F32).reshape(n_req, N_HEADS, QK_DIM), 2, axis=1)
    vmem = pl.BlockSpec(memory_space=pltpu.VMEM)
    hbm = pl.BlockSpec(memory_space=pl.ANY)
    chunk_rows = PAGES_PER_CHUNK * page_rows
    out = pl.pallas_call(
        functools.partial(_decode_attn_kernel, n_pages=n_pages, pages_per_chunk=PAGES_PER_CHUNK),
        grid_spec=pltpu.PrefetchScalarGridSpec(
            num_scalar_prefetch=1,
            grid=(1,),
            in_specs=[vmem, vmem, vmem, vmem, vmem, hbm, hbm],
            out_specs=vmem,
            scratch_shapes=[pltpu.VMEM((2, chunk_rows, QK_DIM), cache_k.dtype),
                            pltpu.VMEM((2, chunk_rows, V_DIM), cache_v.dtype),
                            pltpu.SemaphoreType.DMA((2, 2))],
        ),
        out_shape=jax.ShapeDtypeStruct((n_req, N_HEADS, V_DIM), F32),
        compiler_params=pltpu.CompilerParams(
            dimension_semantics=("arbitrary",), vmem_limit_bytes=VMEM_LIMIT_BYTES),
        name="decode_attn",
    )(page_table.reshape(-1), per_map(q), per_map(k_new), per_map(v_new), lamp, subln_g,
      cache_k, cache_v)
    return out.reshape(n_req, N_HEADS * V_DIM)


def _merge_ffn_kernel(x_ref, ao_ref, u_ref, vs_ref, sga_ref, sgb_ref, ws_ref, bs_ref,
                      wao_ref, wso_ref, wo_ref, gffn_ref, wg_ref, wu_ref, wd_ref, gfin_ref,
                      y_ref, *, chunked):
    tile = x_ref.shape[0]
    sw = u_ref.shape[1]
    gd = sw // SG_GROUPS
    u = u_ref[...].astype(F32)
    if chunked:
        vs = vs_ref[...].astype(BF16)
        n_chunks = tile // CHUNK
        tri = (lax.broadcasted_iota(jnp.int32, (CHUNK, CHUNK), 0)
               >= lax.broadcasted_iota(jnp.int32, (CHUNK, CHUNK), 1))
        groups = []
        for g in range(SG_GROUPS):
            wg = jnp.where(tri, ws_ref[g], 0.0).astype(BF16)
            vg = jnp.concatenate([vs[c * CHUNK:(c + 1) * CHUNK, g * gd:(g + 1) * gd]
                                  for c in range(n_chunks)], axis=1)
            sg = jnp.dot(wg, vg, preferred_element_type=F32) + bs_ref[:, g:g + 1]
            groups.append(jnp.concatenate([sg[:, c * gd:(c + 1) * gd] for c in range(n_chunks)],
                                          axis=0))
        s = jnp.concatenate(groups, axis=1)
    else:
        s = vs_ref[...] * ws_ref[...] + bs_ref[...]
    so = (u * s).astype(BF16)
    attn = jnp.dot(ao_ref[...].astype(BF16), wao_ref[...], preferred_element_type=F32)
    gate = jnp.dot(so, wso_ref[...], preferred_element_type=F32)
    m = sga_ref[...].astype(F32) * attn + sgb_ref[...].astype(F32) * gate
    h = x_ref[...] + jnp.dot(m.astype(BF16), wo_ref[...], preferred_element_type=F32)
    hn = _rmsnorm(h, gffn_ref[...]).astype(BF16)
    d_ff = wg_ref.shape[1]
    step = d_ff // FFN_CHUNKS
    for c in range(FFN_CHUNKS):
        cols = slice(c * step, (c + 1) * step)
        a = jnp.dot(hn, wg_ref[:, cols], preferred_element_type=F32)
        b = jnp.dot(hn, wu_ref[:, cols], preferred_element_type=F32)
        f = (a * _sigmoid(a) * b).astype(BF16)
        h = h + jnp.dot(f, wd_ref[cols, :], preferred_element_type=F32)
    y_ref[...] = _rmsnorm(h, gfin_ref[...])


def _merge_ffn(x, ao, u, vs, sga, sgb, ws, bs, wao, wso, wo, gffn, wg, wu, wd, gfin, *, tile, chunked):
    n, dm = x.shape
    assert n % tile == 0 and wg.shape[1] % (FFN_CHUNKS * LANES) == 0
    row = lambda a: pl.BlockSpec((tile, a.shape[1]), lambda i: (i, 0))
    consts = (ws, bs, wao, wso, wo, gffn, wg, wu, wd, gfin)
    return pl.pallas_call(
        functools.partial(_merge_ffn_kernel, chunked=chunked),
        grid=(n // tile,),
        in_specs=[row(a) for a in (x, ao, u, vs, sga, sgb)] + [_const_spec(a.shape) for a in consts],
        out_specs=pl.BlockSpec((tile, dm), lambda i: (i, 0)),
        out_shape=jax.ShapeDtypeStruct((n, dm), F32),
        compiler_params=pltpu.CompilerParams(
            dimension_semantics=("arbitrary",), vmem_limit_bytes=VMEM_LIMIT_BYTES),
        name="merge_ffn_chunked" if chunked else "merge_ffn_rows",
    )(x, ao, u, vs, sga, sgb, *consts)


def kernel(x_prompt, x_sample, cache_k, cache_v, page_table, w_in, lam_q1, lam_k1, lam_q2, lam_k2,
           subln_g, sg_ln_g, sg_ln_b, w_spatial, b_spatial, w_attn_out, w_sg_out, w_o,
           norm_mix_g, norm_ffn_g, w_gate, w_up, w_down, norm_final_g):
    assert w_in.shape[0] == 1 and cache_k.shape[0] == 1, "single-layer trunk"
    b, s, dm = x_prompt.shape
    bd, t_s, _ = x_sample.shape
    assert t_s == 1 and s % CHUNK == 0
    page = cache_k.shape[2]
    assert cache_k.shape[3:] == (N_HEADS, QK_DIM) and cache_v.shape[3:] == (N_HEADS, V_DIM)
    qw, aw = N_HEADS * QK_DIM, N_HEADS * V_DIM
    sw = sg_ln_g.shape[-1]
    gd = sw // SG_GROUPS

    row1 = lambda a: a.reshape(1, -1)
    win = w_in[0].astype(BF16)
    gmix, lng, lnb = row1(norm_mix_g[0]), row1(sg_ln_g[0]), row1(sg_ln_b[0])
    lamp = jnp.stack([lam_q1[0], lam_k1[0], lam_q2[0], lam_k2[0]])
    subg = row1(subln_g[0])
    merge_w = (w_attn_out[0].astype(BF16), w_sg_out[0].astype(BF16), w_o[0].astype(BF16),
               row1(norm_ffn_g[0]), w_gate[0].astype(BF16), w_up[0].astype(BF16),
               w_down[0].astype(BF16), row1(norm_final_g))

    xp = x_prompt.reshape(b * s, dm)
    xs = x_sample.reshape(bd, dm)

    qp, kp, vp, kbp, vbp, up, vsp, sgap, sgbp = _inproj(
        xp, gmix, win, lng, lnb, tile=ROW_TILE, emit_bf16_kv=True)
    qs, ks, vs_, us, vss, sgas, sgbs = _inproj(
        xs, gmix, win, lng, lnb, tile=bd, emit_bf16_kv=False)

    ao_p = _prompt_attn(qp.reshape(b, s, qw), kbp.reshape(b, s, qw), vbp.reshape(b, s, aw),
                        lamp, subg)
    ao_s = _decode_attn(qs, ks, vs_, cache_k.reshape(-1, QK_DIM), cache_v.reshape(-1, V_DIM),
                        page_table, lamp, subg, page_rows=page * N_HEADS)

    yp = _merge_ffn(xp, ao_p.reshape(b * s, aw), up, vsp, sgap, sgbp,
                    w_spatial[0], jnp.transpose(b_spatial[0]), *merge_w,
                    tile=ROW_TILE, chunked=True)
    ys = _merge_ffn(xs, ao_s, us, vss, sgas, sgbs,
                    row1(jnp.repeat(w_spatial[0, :, 0, 0], gd)), row1(jnp.repeat(b_spatial[0, :, 0], gd)),
                    *merge_w, tile=bd, chunked=False)

    y_prompt = yp.reshape(b, s, dm)
    y_sample = ys.reshape(bd, 1, dm)
    k_prompt = kp.reshape(1, b, s, N_HEADS, QK_DIM)
    v_prompt = vp.reshape(1, b, s, N_HEADS, V_DIM)
    k_sample = ks.reshape(1, bd, 1, N_HEADS, QK_DIM)
    v_sample = vs_.reshape(1, bd, 1, N_HEADS, V_DIM)
    sg_v_prompt = vsp.reshape(b, s, sw)[:, s - CHUNK:, :].reshape(1, b, CHUNK, SG_GROUPS, gd)
    sg_v_sample = vss.reshape(1, bd, 1, SG_GROUPS, gd)
    return (y_prompt, y_sample, k_prompt, v_prompt, k_sample, v_sample, sg_v_prompt, sg_v_sample)
```

```python
import functools
import math

import jax
import jax.numpy as jnp
from jax import lax
from jax.experimental import pallas as pl
from jax.experimental.pallas import tpu as pltpu

F32 = jnp.float32
BF16 = jnp.bfloat16

N_HEADS = 4
HEAD_DIM = 64
QK_DIM = 2 * HEAD_DIM
V_DIM = 2 * HEAD_DIM
N_SCORE_ROWS = 2 * N_HEADS
SG_GROUPS = 4
CHUNK = 128
EPS = 1e-6
QK_SCALE = HEAD_DIM ** -0.5
LAM_INIT = 0.8 - 0.6 * math.exp(-0.3 * 0)

LANES = 128
VMEM_LIMIT_BYTES = 60 * 1024 * 1024

ROW_TILE = 256
ATTN_Q_TILE = 256
PAGES_PER_CHUNK = 16
KV_SLOTS = 2
FFN_CHUNKS = 2


def _const_spec(shape):
    zeros = (0,) * len(shape)
    return pl.BlockSpec(shape, lambda *_: zeros, pipeline_mode=pl.Buffered(1))


def _rmsnorm(x, g):
    return x * lax.rsqrt(jnp.mean(x * x, axis=-1, keepdims=True) + EPS) * g


def _gelu(x):
    return 0.5 * x * (1.0 + lax.erf(x * math.sqrt(0.5)))


def _sigmoid(x):
    return 1.0 / (1.0 + jnp.exp(-x))


def _lam(lamp_ref):
    p = lamp_ref[...]
    a = jnp.sum(p[0:1] * p[1:2], axis=-1, keepdims=True)
    b = jnp.sum(p[2:3] * p[3:4], axis=-1, keepdims=True)
    return jnp.exp(a) - jnp.exp(b) + LAM_INIT


def _subln(o, gain):
    return o * lax.rsqrt(jnp.mean(o * o, axis=-1, keepdims=True) + EPS) * gain


def _inproj_kernel(x_ref, gmix_ref, win_ref, lng_ref, lnb_ref, *out_refs, widths, emit_bf16_kv):
    qw, aw, sw, dm = widths
    if emit_bf16_kv:
        q_ref, k_ref, v_ref, kb_ref, vb_ref, u_ref, vs_ref, sga_ref, sgb_ref = out_refs
    else:
        q_ref, k_ref, v_ref, u_ref, vs_ref, sga_ref, sgb_ref = out_refs
    tile = x_ref.shape[0]
    xn = _rmsnorm(x_ref[...], gmix_ref[...]).astype(BF16)

    def proj(lo, width):
        return jnp.dot(xn, win_ref[:, lo:lo + width], preferred_element_type=F32)

    def store_by_head(ref, val, head_dim):
        for h in range(N_HEADS):
            ref[pl.ds(h, tile, stride=N_HEADS), :] = val[:, h * head_dim:(h + 1) * head_dim]

    q_ref[...] = (proj(0, qw) * QK_SCALE).astype(BF16)
    k = proj(qw, qw)
    store_by_head(k_ref, k, QK_DIM)
    v = proj(2 * qw, aw)
    store_by_head(v_ref, v, V_DIM)
    if emit_bf16_kv:
        kb_ref[...] = k.astype(BF16)
        vb_ref[...] = v.astype(BF16)
    off = 2 * qw + aw
    u_ref[...] = _gelu(proj(off, sw)).astype(BF16)
    gv = _gelu(proj(off + sw, sw))
    mu = jnp.mean(gv, axis=-1, keepdims=True)
    gc = gv - mu
    var = jnp.mean(gc * gc, axis=-1, keepdims=True)
    vs_ref[...] = gc * lax.rsqrt(var + EPS) * lng_ref[...] + lnb_ref[...]
    off += 2 * sw
    sga_ref[...] = _sigmoid(proj(off, dm)).astype(BF16)
    sgb_ref[...] = _sigmoid(proj(off + dm, dm)).astype(BF16)


def _inproj(x, gmix, win, lng, lnb, *, tile, emit_bf16_kv):
    n, dm = x.shape
    sw = lng.shape[-1]
    qw = N_HEADS * QK_DIM
    aw = N_HEADS * V_DIM
    assert win.shape == (dm, 2 * qw + aw + 2 * sw + 2 * dm)
    assert n % tile == 0
    row = lambda w: pl.BlockSpec((tile, w), lambda i: (i, 0))
    by_head = lambda w: pl.BlockSpec((tile * N_HEADS, w), lambda i: (i, 0))
    outs = [(n, qw, BF16, row(qw)),
            (n * N_HEADS, QK_DIM, F32, by_head(QK_DIM)),
            (n * N_HEADS, V_DIM, F32, by_head(V_DIM))]
    if emit_bf16_kv:
        outs += [(n, qw, BF16, row(qw)), (n, aw, BF16, row(aw))]
    outs += [(n, sw, BF16, row(sw)), (n, sw, F32, row(sw)),
             (n, dm, BF16, row(dm)), (n, dm, BF16, row(dm))]
    return pl.pallas_call(
        functools.partial(_inproj_kernel, widths=(qw, aw, sw, dm), emit_bf16_kv=emit_bf16_kv),
        grid=(n // tile,),
        in_specs=[row(dm), _const_spec((1, dm)), _const_spec(win.shape),
                  _const_spec((1, sw)), _const_spec((1, sw))],
        out_specs=[o[3] for o in outs],
        out_shape=[jax.ShapeDtypeStruct((o[0], o[1]), o[2]) for o in outs],
        compiler_params=pltpu.CompilerParams(
            dimension_semantics=("arbitrary",), vmem_limit_bytes=VMEM_LIMIT_BYTES),
        name="inproj",
    )(x, gmix, win, lng, lnb)


def _prompt_attn_kernel(q_ref, k_ref, v_ref, lamp_ref, sg_ref, o_ref, *, tq):
    seq = q_ref.shape[1]
    lam = _lam(lamp_ref)
    gain = sg_ref[...] * (1.0 - LAM_INIT)
    lane = lax.broadcasted_iota(jnp.int32, (tq, QK_DIM), 1)
    first_map = lane < HEAD_DIM
    row = lax.broadcasted_iota(jnp.int32, (2 * tq, tq), 0)
    col = lax.broadcasted_iota(jnp.int32, (2 * tq, tq), 1)
    causal = col <= jnp.where(row >= tq, row - tq, row)
    for i in range(seq // tq):
        kv = (i + 1) * tq
        q = q_ref[0, i * tq:(i + 1) * tq, :]
        zero = jnp.zeros_like(q)
        qq = jnp.concatenate([jnp.where(first_map, q, zero), jnp.where(first_map, zero, q)], axis=0)
        s = lax.dot_general(qq, k_ref[0, :kv, :], (((1,), (1,)), ((), ())),
                            preferred_element_type=F32)
        s_diag = jnp.where(causal, s[:, kv - tq:], -jnp.inf)
        s = s_diag if i == 0 else jnp.concatenate([s[:, :kv - tq], s_diag], axis=1)
        m = jnp.max(s, axis=-1, keepdims=True)
        p = jnp.exp(s - m)
        r = 1.0 / jnp.sum(p, axis=-1, keepdims=True)
        w = p[:tq] * r[:tq] - lam * (p[tq:] * r[tq:])
        o = jnp.dot(w.astype(BF16), v_ref[0, :kv, :], preferred_element_type=F32)
        o_ref[0, i * tq:(i + 1) * tq, :] = _subln(o, gain).astype(o_ref.dtype)


def _prompt_attn(q, k, v, lamp, subln_g):
    b, s, _ = q.shape
    head = pl.BlockSpec((1, s, QK_DIM), lambda bi, hi: (bi, 0, hi))
    return pl.pallas_call(
        functools.partial(_prompt_attn_kernel, tq=ATTN_Q_TILE),
        grid=(b, N_HEADS),
        in_specs=[head, head, head, _const_spec(lamp.shape), _const_spec(subln_g.shape)],
        out_specs=pl.BlockSpec((1, s, V_DIM), lambda bi, hi: (bi, 0, hi)),
        out_shape=jax.ShapeDtypeStruct((b, s, N_HEADS * V_DIM), BF16),
        compiler_params=pltpu.CompilerParams(
            dimension_semantics=("arbitrary", "arbitrary"), vmem_limit_bytes=VMEM_LIMIT_BYTES),
        name="prompt_attn",
    )(q, k, v, lamp, subln_g)


def _decode_chunk_steps(pt_ref, q_ref, kn_ref, vn_ref, lamp_ref, sg_ref, ck_hbm, cv_hbm,
                        o_ref, kbuf, vbuf, sem, *, n_pages, pages_per_chunk):
    step, n_steps = pl.program_id(0), pl.num_programs(0)
    rps, n_rows, _ = q_ref.shape
    chunk_rows = kbuf.shape[1]
    page_rows = chunk_rows // pages_per_chunk
    n_chunks = n_pages // pages_per_chunk
    per_step = rps * n_chunks
    assert per_step % KV_SLOTS == 0

    def page_copies(r, c, slot, p):
        idx = pt_ref[r * n_pages + c * pages_per_chunk + p]
        src = pl.ds(pl.multiple_of(idx * page_rows, page_rows), page_rows)
        dst = pl.ds(p * page_rows, page_rows)
        return (pltpu.make_async_copy(ck_hbm.at[src], kbuf.at[slot, dst], sem.at[slot, 0]),
                pltpu.make_async_copy(cv_hbm.at[src], vbuf.at[slot, dst], sem.at[slot, 1]))

    def start_chunk(r, c, slot):
        for p in range(pages_per_chunk):
            for cp in page_copies(r, c, slot, p):
                cp.start()

    def wait_chunk(r, c, slot):
        for p in range(pages_per_chunk):
            for cp in page_copies(r, c, slot, p):
                cp.wait()

    lam = _lam(lamp_ref)
    gain = sg_ref[...] * (1.0 - LAM_INIT)
    own_map = (lax.broadcasted_iota(jnp.int32, (n_rows, QK_DIM), 1) // HEAD_DIM
               == lax.broadcasted_iota(jnp.int32, (n_rows, QK_DIM), 0) % 2)
    own_head = (lax.broadcasted_iota(jnp.int32, (n_rows, chunk_rows), 1) % N_HEADS
                == lax.broadcasted_iota(jnp.int32, (n_rows, chunk_rows), 0) // 2)

    @pl.when(step == 0)
    def _():
        start_chunk(0, 0, 0)

    state = {}

    def chunk_step(g):
        j, c = divmod(g, n_chunks)
        slot = g % KV_SLOTS
        r = step * rps + j
        if g + 1 < per_step:
            jn, cn = divmod(g + 1, n_chunks)
            start_chunk(step * rps + jn, cn, (g + 1) % KV_SLOTS)
        else:
            @pl.when(step + 1 < n_steps)
            def _():
                start_chunk((step + 1) * rps, 0, 0)
        if c == 0:
            qb = jnp.where(own_map, q_ref[j], 0.0).astype(BF16)
            kn = kn_ref[j].astype(BF16).astype(F32)
            m = jnp.sum(qb.astype(F32) * kn, axis=-1, keepdims=True)
            state.update(qb=qb, m=m, l=jnp.ones_like(m),
                         acc=vn_ref[j].astype(BF16).astype(F32))
        wait_chunk(r, c, slot)
        kb = kbuf[slot].astype(BF16)
        s = lax.dot_general(state["qb"], kb, (((1,), (1,)), ((), ())), preferred_element_type=F32)
        s = jnp.where(own_head, s, -jnp.inf)
        m_new = jnp.maximum(state["m"], jnp.max(s, axis=-1, keepdims=True))
        alpha = jnp.exp(state["m"] - m_new)
        p = jnp.exp(s - m_new)
        state["l"] = alpha * state["l"] + jnp.sum(p, axis=-1, keepdims=True)
        state["acc"] = alpha * state["acc"] + jnp.dot(
            p.astype(BF16), vbuf[slot].astype(BF16), preferred_element_type=F32)
        state["m"] = m_new
        if c == n_chunks - 1:
            a = state["acc"] * (1.0 / state["l"])
            for h in range(N_HEADS):
                o = a[2 * h:2 * h + 1] - lam * a[2 * h + 1:2 * h + 2]
                o_ref[j, h:h + 1, :] = _subln(o, gain)

    return [functools.partial(chunk_step, g) for g in range(per_step)]


def _merge_ffn_pieces(x_ref, ao_ref, u_ref, vs_ref, sga_ref, sgb_ref, ws_ref, bs_ref,
                      wao_ref, wso_ref, wo_ref, gffn_ref, wg_ref, wu_ref, wd_ref, gfin_ref,
                      y_ref, *, chunked):
    tile = x_ref.shape[0]
    sw = u_ref.shape[1]
    gd = sw // SG_GROUPS
    u = u_ref[...].astype(F32)
    if chunked:
        vs = vs_ref[...].astype(BF16)
        n_chunks = tile // CHUNK
        tri = (lax.broadcasted_iota(jnp.int32, (CHUNK, CHUNK), 0)
               >= lax.broadcasted_iota(jnp.int32, (CHUNK, CHUNK), 1))
        groups = []
        for g in range(SG_GROUPS):
            wg = jnp.where(tri, ws_ref[g], 0.0).astype(BF16)
            vg = jnp.concatenate([vs[c * CHUNK:(c + 1) * CHUNK, g * gd:(g + 1) * gd]
                                  for c in range(n_chunks)], axis=1)
            sg = jnp.dot(wg, vg, preferred_element_type=F32) + bs_ref[:, g:g + 1]
            groups.append(jnp.concatenate([sg[:, c * gd:(c + 1) * gd] for c in range(n_chunks)],
                                          axis=0))
        s = jnp.concatenate(groups, axis=1)
    else:
        s = vs_ref[...] * ws_ref[...] + bs_ref[...]
    so = (u * s).astype(BF16)
    yield
    attn = jnp.dot(ao_ref[...].astype(BF16), wao_ref[...], preferred_element_type=F32)
    gate = jnp.dot(so, wso_ref[...], preferred_element_type=F32)
    m = sga_ref[...].astype(F32) * attn + sgb_ref[...].astype(F32) * gate
    yield
    h = x_ref[...] + jnp.dot(m.astype(BF16), wo_ref[...], preferred_element_type=F32)
    hn = _rmsnorm(h, gffn_ref[...]).astype(BF16)
    yield
    d_ff = wg_ref.shape[1]
    step = d_ff // FFN_CHUNKS
    for c in range(FFN_CHUNKS):
        cols = slice(c * step, (c + 1) * step)
        a = jnp.dot(hn, wg_ref[:, cols], preferred_element_type=F32)
        yield
        b = jnp.dot(hn, wu_ref[:, cols], preferred_element_type=F32)
        f = (a * _sigmoid(a) * b).astype(BF16)
        yield
        h = h + jnp.dot(f, wd_ref[cols, :], preferred_element_type=F32)
        if c + 1 < FFN_CHUNKS:
            yield
    y_ref[...] = _rmsnorm(h, gfin_ref[...])


N_MERGE_IN = 16
N_DECODE_IN = 7


def _merge_ffn_kernel(*refs, chunked):
    for _ in _merge_ffn_pieces(*refs, chunked=chunked):
        pass


def _merge_ffn_decode_kernel(pt_ref, *refs, n_pages, pages_per_chunk):
    merge_in, rest = refs[:N_MERGE_IN], refs[N_MERGE_IN:]
    decode_in, rest = rest[:N_DECODE_IN], rest[N_DECODE_IN:]
    y_ref, o_ref, kbuf, vbuf, sem = rest
    pieces = _merge_ffn_pieces(*merge_in, y_ref, chunked=True)
    for chunk_step in _decode_chunk_steps(pt_ref, *decode_in, o_ref, kbuf, vbuf, sem,
                                          n_pages=n_pages, pages_per_chunk=pages_per_chunk):
        chunk_step()
        next(pieces, None)
    for _ in pieces:
        pass


def _merge_ffn(x, ao, u, vs, sga, sgb, ws, bs, wao, wso, wo, gffn, wg, wu, wd, gfin, *, tile):
    n, dm = x.shape
    assert n % tile == 0 and wg.shape[1] % (FFN_CHUNKS * LANES) == 0
    row = lambda a: pl.BlockSpec((tile, a.shape[1]), lambda i: (i, 0))
    consts = (ws, bs, wao, wso, wo, gffn, wg, wu, wd, gfin)
    return pl.pallas_call(
        functools.partial(_merge_ffn_kernel, chunked=False),
        grid=(n // tile,),
        in_specs=[row(a) for a in (x, ao, u, vs, sga, sgb)] + [_const_spec(a.shape) for a in consts],
        out_specs=pl.BlockSpec((tile, dm), lambda i: (i, 0)),
        out_shape=jax.ShapeDtypeStruct((n, dm), F32),
        compiler_params=pltpu.CompilerParams(
            dimension_semantics=("arbitrary",), vmem_limit_bytes=VMEM_LIMIT_BYTES),
        name="merge_ffn_rows",
    )(x, ao, u, vs, sga, sgb, *consts)


def _merge_ffn_with_decode(x, ao, u, vs, sga, sgb, ws, bs, wao, wso, wo, gffn, wg, wu, wd, gfin,
                           q, k_new, v_new, cache_k, cache_v, page_table, lamp, subln_g,
                           *, tile, page_rows):
    n, dm = x.shape
    n_req, n_pages = page_table.shape
    n_steps = n // tile
    assert n % tile == 0 and tile % CHUNK == 0 and wg.shape[1] % (FFN_CHUNKS * LANES) == 0
    assert n_req % n_steps == 0 and n_pages % PAGES_PER_CHUNK == 0
    rps = n_req // n_steps
    per_map = lambda a: jnp.repeat(a.astype(F32).reshape(n_req, N_HEADS, QK_DIM), 2, axis=1)
    row = lambda a: pl.BlockSpec((tile, a.shape[1]), lambda i, pt: (i, 0))
    req = lambda w: pl.BlockSpec((rps, N_SCORE_ROWS, w), lambda i, pt: (i, 0, 0))
    hbm = pl.BlockSpec(memory_space=pl.ANY)
    consts = (ws, bs, wao, wso, wo, gffn, wg, wu, wd, gfin)
    chunk_rows = PAGES_PER_CHUNK * page_rows
    y, o = pl.pallas_call(
        functools.partial(_merge_ffn_decode_kernel, n_pages=n_pages, pages_per_chunk=PAGES_PER_CHUNK),
        grid_spec=pltpu.PrefetchScalarGridSpec(
            num_scalar_prefetch=1,
            grid=(n_steps,),
            in_specs=([row(a) for a in (x, ao, u, vs, sga, sgb)]
                      + [_const_spec(a.shape) for a in consts]
                      + [req(QK_DIM), req(QK_DIM), req(V_DIM),
                         _const_spec(lamp.shape), _const_spec(subln_g.shape), hbm, hbm]),
            out_specs=[pl.BlockSpec((tile, dm), lambda i, pt: (i, 0)),
                       pl.BlockSpec((rps, N_HEADS, V_DIM), lambda i, pt: (i, 0, 0))],
            scratch_shapes=[pltpu.VMEM((KV_SLOTS, chunk_rows, QK_DIM), cache_k.dtype),
                            pltpu.VMEM((KV_SLOTS, chunk_rows, V_DIM), cache_v.dtype),
                            pltpu.SemaphoreType.DMA((KV_SLOTS, 2))],
        ),
        out_shape=[jax.ShapeDtypeStruct((n, dm), F32),
                   jax.ShapeDtypeStruct((n_req, N_HEADS, V_DIM), F32)],
        compiler_params=pltpu.CompilerParams(
            dimension_semantics=("arbitrary",), vmem_limit_bytes=VMEM_LIMIT_BYTES),
        name="merge_ffn_decode",
    )(page_table.reshape(-1), x, ao, u, vs, sga, sgb, *consts,
      per_map(q), per_map(k_new), per_map(v_new), lamp, subln_g, cache_k, cache_v)
    return y, o.reshape(n_req, N_HEADS * V_DIM)


def kernel(x_prompt, x_sample, cache_k, cache_v, page_table, w_in, lam_q1, lam_k1, lam_q2, lam_k2,
           subln_g, sg_ln_g, sg_ln_b, w_spatial, b_spatial, w_attn_out, w_sg_out, w_o,
           norm_mix_g, norm_ffn_g, w_gate, w_up, w_down, norm_final_g):
    assert w_in.shape[0] == 1 and cache_k.shape[0] == 1, "single-layer trunk"
    b, s, dm = x_prompt.shape
    bd, t_s, _ = x_sample.shape
    assert t_s == 1 and s % CHUNK == 0
    page = cache_k.shape[2]
    assert cache_k.shape[3:] == (N_HEADS, QK_DIM) and cache_v.shape[3:] == (N_HEADS, V_DIM)
    qw, aw = N_HEADS * QK_DIM, N_HEADS * V_DIM
    sw = sg_ln_g.shape[-1]
    gd = sw // SG_GROUPS

    row1 = lambda a: a.reshape(1, -1)
    win = w_in[0].astype(BF16)
    gmix, lng, lnb = row1(norm_mix_g[0]), row1(sg_ln_g[0]), row1(sg_ln_b[0])
    lamp = jnp.stack([lam_q1[0], lam_k1[0], lam_q2[0], lam_k2[0]])
    subg = row1(subln_g[0])
    merge_w = (w_attn_out[0].astype(BF16), w_sg_out[0].astype(BF16), w_o[0].astype(BF16),
               row1(norm_ffn_g[0]), w_gate[0].astype(BF16), w_up[0].astype(BF16),
               w_down[0].astype(BF16), row1(norm_final_g))

    xp = x_prompt.reshape(b * s, dm)
    xs = x_sample.reshape(bd, dm)

    qs, ks, vs_, us, vss, sgas, sgbs = _inproj(
        xs, gmix, win, lng, lnb, tile=bd, emit_bf16_kv=False)
    qp, kp, vp, kbp, vbp, up, vsp, sgap, sgbp = _inproj(
        xp, gmix, win, lng, lnb, tile=ROW_TILE, emit_bf16_kv=True)

    ao_p = _prompt_attn(qp.reshape(b, s, qw), kbp.reshape(b, s, qw), vbp.reshape(b, s, aw),
                        lamp, subg)

    yp, ao_s = _merge_ffn_with_decode(
        xp, ao_p.reshape(b * s, aw), up, vsp, sgap, sgbp,
        w_spatial[0], jnp.transpose(b_spatial[0]), *merge_w,
        qs, ks, vs_, cache_k.reshape(-1, QK_DIM), cache_v.reshape(-1, V_DIM), page_table,
        lamp, subg, tile=ROW_TILE, page_rows=page * N_HEADS)
    ys = _merge_ffn(xs, ao_s, us, vss, sgas, sgbs,
                    row1(jnp.repeat(w_spatial[0, :, 0, 0], gd)), row1(jnp.repeat(b_spatial[0, :, 0], gd)),
                    *merge_w, tile=bd)

    y_prompt = yp.reshape(b, s, dm)
    y_sample = ys.reshape(bd, 1, dm)
    k_prompt = kp.reshape(1, b, s, N_HEADS, QK_DIM)
    v_prompt = vp.reshape(1, b, s, N_HEADS, V_DIM)
    k_sample = ks.reshape(1, bd, 1, N_HEADS, QK_DIM)
    v_sample = vs_.reshape(1, bd, 1, N_HEADS, V_DIM)
    sg_v_prompt = vsp.reshape(b, s, sw)[:, s - CHUNK:, :].reshape(1, b, CHUNK, SG_GROUPS, gd)
    sg_v_sample = vss.reshape(1, bd, 1, SG_GROUPS, gd)
    return (y_prompt, y_sample, k_prompt, v_prompt, k_sample, v_sample, sg_v_prompt, sg_v_sample)
```

```python
import functools
import math

import jax
import jax.numpy as jnp
from jax import lax
from jax.experimental import pallas as pl
from jax.experimental.pallas import tpu as pltpu

F32 = jnp.float32
BF16 = jnp.bfloat16

N_HEADS = 4
HEAD_DIM = 64
QK_DIM = 2 * HEAD_DIM
V_DIM = 2 * HEAD_DIM
N_SCORE_ROWS = 2 * N_HEADS
SG_GROUPS = 4
CHUNK = 128
EPS = 1e-6
QK_SCALE = HEAD_DIM ** -0.5
LAM_INIT = 0.8 - 0.6 * math.exp(-0.3 * 0)

LANES = 128
VMEM_LIMIT_BYTES = 60 * 1024 * 1024

ROW_TILE = 256
ATTN_Q_TILE = 256
PAGES_PER_CHUNK = 16
KV_SLOTS = 2
FFN_CHUNKS = 2
ATTN_HOSTED_REQUEST_SHARE = 0.5


def _const_spec(shape):
    zeros = (0,) * len(shape)
    return pl.BlockSpec(shape, lambda *_: zeros, pipeline_mode=pl.Buffered(1))


def _rmsnorm(x, g):
    return x * lax.rsqrt(jnp.mean(x * x, axis=-1, keepdims=True) + EPS) * g


def _gelu(x):
    return 0.5 * x * (1.0 + lax.erf(x * math.sqrt(0.5)))


def _sigmoid(x):
    return 1.0 / (1.0 + jnp.exp(-x))


def _lam(lamp_ref):
    p = lamp_ref[...]
    a = jnp.sum(p[0:1] * p[1:2], axis=-1, keepdims=True)
    b = jnp.sum(p[2:3] * p[3:4], axis=-1, keepdims=True)
    return jnp.exp(a) - jnp.exp(b) + LAM_INIT


def _subln(o, gain):
    return o * lax.rsqrt(jnp.mean(o * o, axis=-1, keepdims=True) + EPS) * gain


def _inproj_kernel(x_ref, gmix_ref, win_ref, lng_ref, lnb_ref, *out_refs, widths, emit_bf16_kv):
    qw, aw, sw, dm = widths
    if emit_bf16_kv:
        q_ref, k_ref, v_ref, kb_ref, vb_ref, u_ref, vs_ref, sga_ref, sgb_ref = out_refs
    else:
        q_ref, k_ref, v_ref, u_ref, vs_ref, sga_ref, sgb_ref = out_refs
    tile = x_ref.shape[0]
    xn = _rmsnorm(x_ref[...], gmix_ref[...]).astype(BF16)

    def proj(lo, width):
        return jnp.dot(xn, win_ref[:, lo:lo + width], preferred_element_type=F32)

    def store_by_head(ref, val, head_dim):
        for h in range(N_HEADS):
            ref[pl.ds(h, tile, stride=N_HEADS), :] = val[:, h * head_dim:(h + 1) * head_dim]

    q_ref[...] = (proj(0, qw) * QK_SCALE).astype(BF16)
    k = proj(qw, qw)
    store_by_head(k_ref, k, QK_DIM)
    v = proj(2 * qw, aw)
    store_by_head(v_ref, v, V_DIM)
    if emit_bf16_kv:
        kb_ref[...] = k.astype(BF16)
        vb_ref[...] = v.astype(BF16)
    off = 2 * qw + aw
    u_ref[...] = _gelu(proj(off, sw)).astype(BF16)
    gv = _gelu(proj(off + sw, sw))
    mu = jnp.mean(gv, axis=-1, keepdims=True)
    gc = gv - mu
    var = jnp.mean(gc * gc, axis=-1, keepdims=True)
    vs_ref[...] = gc * lax.rsqrt(var + EPS) * lng_ref[...] + lnb_ref[...]
    off += 2 * sw
    sga_ref[...] = _sigmoid(proj(off, dm)).astype(BF16)
    sgb_ref[...] = _sigmoid(proj(off + dm, dm)).astype(BF16)


def _inproj(x, gmix, win, lng, lnb, *, tile, emit_bf16_kv):
    n, dm = x.shape
    sw = lng.shape[-1]
    qw = N_HEADS * QK_DIM
    aw = N_HEADS * V_DIM
    assert win.shape == (dm, 2 * qw + aw + 2 * sw + 2 * dm)
    assert n % tile == 0
    row = lambda w: pl.BlockSpec((tile, w), lambda i: (i, 0))
    by_head = lambda w: pl.BlockSpec((tile * N_HEADS, w), lambda i: (i, 0))
    outs = [(n, qw, BF16, row(qw)),
            (n * N_HEADS, QK_DIM, F32, by_head(QK_DIM)),
            (n * N_HEADS, V_DIM, F32, by_head(V_DIM))]
    if emit_bf16_kv:
        outs += [(n, qw, BF16, row(qw)), (n, aw, BF16, row(aw))]
    outs += [(n, sw, BF16, row(sw)), (n, sw, F32, row(sw)),
             (n, dm, BF16, row(dm)), (n, dm, BF16, row(dm))]
    return pl.pallas_call(
        functools.partial(_inproj_kernel, widths=(qw, aw, sw, dm), emit_bf16_kv=emit_bf16_kv),
        grid=(n // tile,),
        in_specs=[row(dm), _const_spec((1, dm)), _const_spec(win.shape),
                  _const_spec((1, sw)), _const_spec((1, sw))],
        out_specs=[o[3] for o in outs],
        out_shape=[jax.ShapeDtypeStruct((o[0], o[1]), o[2]) for o in outs],
        compiler_params=pltpu.CompilerParams(
            dimension_semantics=("arbitrary",), vmem_limit_bytes=VMEM_LIMIT_BYTES),
        name="inproj",
    )(x, gmix, win, lng, lnb)


N_DECODE_IN = 7


def _decode_chunk_steps(pt_ref, q_ref, kn_ref, vn_ref, lamp_ref, sg_ref, ck_hbm, cv_hbm,
                        o_ref, kbuf, vbuf, sem, *, step, n_steps, n_pages):
    rps, n_rows, _ = q_ref.shape
    chunk_rows = kbuf.shape[1]
    page_rows = chunk_rows // PAGES_PER_CHUNK
    n_chunks = n_pages // PAGES_PER_CHUNK
    per_step = rps * n_chunks
    assert per_step % KV_SLOTS == 0

    def page_copies(r, c, slot, p):
        idx = pt_ref[r * n_pages + c * PAGES_PER_CHUNK + p]
        src = pl.ds(pl.multiple_of(idx * page_rows, page_rows), page_rows)
        dst = pl.ds(p * page_rows, page_rows)
        return (pltpu.make_async_copy(ck_hbm.at[src], kbuf.at[slot, dst], sem.at[slot, 0]),
                pltpu.make_async_copy(cv_hbm.at[src], vbuf.at[slot, dst], sem.at[slot, 1]))

    def start_chunk(r, c, slot):
        for p in range(PAGES_PER_CHUNK):
            for cp in page_copies(r, c, slot, p):
                cp.start()

    def wait_chunk(r, c, slot):
        for p in range(PAGES_PER_CHUNK):
            for cp in page_copies(r, c, slot, p):
                cp.wait()

    lam = _lam(lamp_ref)
    gain = sg_ref[...] * (1.0 - LAM_INIT)
    own_map = (lax.broadcasted_iota(jnp.int32, (n_rows, QK_DIM), 1) // HEAD_DIM
               == lax.broadcasted_iota(jnp.int32, (n_rows, QK_DIM), 0) % 2)
    own_head = (lax.broadcasted_iota(jnp.int32, (n_rows, chunk_rows), 1) % N_HEADS
                == lax.broadcasted_iota(jnp.int32, (n_rows, chunk_rows), 0) // 2)

    @pl.when(step == 0)
    def _():
        start_chunk(0, 0, 0)

    state = {}

    def chunk_step(g):
        j, c = divmod(g, n_chunks)
        slot = g % KV_SLOTS
        r = step * rps + j
        if g + 1 < per_step:
            jn, cn = divmod(g + 1, n_chunks)
            start_chunk(step * rps + jn, cn, (g + 1) % KV_SLOTS)
        else:
            @pl.when(step + 1 < n_steps)
            def _():
                start_chunk((step + 1) * rps, 0, 0)
        if c == 0:
            qb = jnp.where(own_map, q_ref[j], 0.0).astype(BF16)
            kn = kn_ref[j].astype(BF16).astype(F32)
            m = jnp.sum(qb.astype(F32) * kn, axis=-1, keepdims=True)
            state.update(qb=qb, m=m, l=jnp.ones_like(m),
                         acc=vn_ref[j].astype(BF16).astype(F32))
        wait_chunk(r, c, slot)
        kb = kbuf[slot].astype(BF16)
        s = lax.dot_general(state["qb"], kb, (((1,), (1,)), ((), ())), preferred_element_type=F32)
        s = jnp.where(own_head, s, -jnp.inf)
        m_new = jnp.maximum(state["m"], jnp.max(s, axis=-1, keepdims=True))
        alpha = jnp.exp(state["m"] - m_new)
        p = jnp.exp(s - m_new)
        state["l"] = alpha * state["l"] + jnp.sum(p, axis=-1, keepdims=True)
        state["acc"] = alpha * state["acc"] + jnp.dot(
            p.astype(BF16), vbuf[slot].astype(BF16), preferred_element_type=F32)
        state["m"] = m_new
        if c == n_chunks - 1:
            a = state["acc"] * (1.0 / state["l"])
            for h in range(N_HEADS):
                o = a[2 * h:2 * h + 1] - lam * a[2 * h + 1:2 * h + 2]
                o_ref[j, h:h + 1, :] = _subln(o, gain)

    return [functools.partial(chunk_step, g) for g in range(per_step)]


def _run_interleaved(chunk_steps, pieces, weights):
    total, n, k, start = sum(weights), len(chunk_steps), 0, 0
    for w in weights:
        while k < n and start * n >= k * total:
            chunk_steps[k]()
            k += 1
        next(pieces, None)
        start += w
    for chunk_step in chunk_steps[k:]:
        chunk_step()
    for _ in pieces:
        raise AssertionError("more pieces than weights")


def _decode_hosting(q, k_new, v_new, cache_k, cache_v, page_table, lamp, subln_g,
                    *, n_steps, step_of, page_rows):
    n_req, n_pages = page_table.shape
    assert n_req % n_steps == 0 and n_pages % PAGES_PER_CHUNK == 0
    rps = n_req // n_steps
    per_map = lambda a: jnp.repeat(a.astype(F32).reshape(n_req, N_HEADS, QK_DIM), 2, axis=1)
    req = lambda rows, w: pl.BlockSpec((rps, rows, w), lambda *a: (step_of(*a[:-1]), 0, 0))
    hbm = pl.BlockSpec(memory_space=pl.ANY)
    chunk_rows = PAGES_PER_CHUNK * page_rows
    return dict(
        prefetch=page_table.reshape(-1),
        in_specs=[req(N_SCORE_ROWS, QK_DIM), req(N_SCORE_ROWS, QK_DIM), req(N_SCORE_ROWS, V_DIM),
                  _const_spec(lamp.shape), _const_spec(subln_g.shape), hbm, hbm],
        operands=[per_map(q), per_map(k_new), per_map(v_new), lamp, subln_g, cache_k, cache_v],
        out_spec=req(N_HEADS, V_DIM),
        out_shape=jax.ShapeDtypeStruct((n_req, N_HEADS, V_DIM), F32),
        scratch=[pltpu.VMEM((KV_SLOTS, chunk_rows, QK_DIM), cache_k.dtype),
                 pltpu.VMEM((KV_SLOTS, chunk_rows, V_DIM), cache_v.dtype),
                 pltpu.SemaphoreType.DMA((KV_SLOTS, 2))],
    )


def _prompt_attn_pieces(q_ref, k_ref, v_ref, lamp_ref, sg_ref, o_ref, *, tq):
    seq = q_ref.shape[1]
    lam = _lam(lamp_ref)
    gain = sg_ref[...] * (1.0 - LAM_INIT)
    lane = lax.broadcasted_iota(jnp.int32, (tq, QK_DIM), 1)
    first_map = lane < HEAD_DIM
    row = lax.broadcasted_iota(jnp.int32, (2 * tq, tq), 0)
    col = lax.broadcasted_iota(jnp.int32, (2 * tq, tq), 1)
    causal = col <= jnp.where(row >= tq, row - tq, row)
    n_blocks = seq // tq
    for i in range(n_blocks):
        kv = (i + 1) * tq
        q = q_ref[0, i * tq:(i + 1) * tq, :]
        zero = jnp.zeros_like(q)
        qq = jnp.concatenate([jnp.where(first_map, q, zero), jnp.where(first_map, zero, q)], axis=0)
        s = lax.dot_general(qq, k_ref[0, :kv, :], (((1,), (1,)), ((), ())),
                            preferred_element_type=F32)
        s_diag = jnp.where(causal, s[:, kv - tq:], -jnp.inf)
        s = s_diag if i == 0 else jnp.concatenate([s[:, :kv - tq], s_diag], axis=1)
        m = jnp.max(s, axis=-1, keepdims=True)
        p = jnp.exp(s - m)
        r = 1.0 / jnp.sum(p, axis=-1, keepdims=True)
        w = p[:tq] * r[:tq] - lam * (p[tq:] * r[tq:])
        o = jnp.dot(w.astype(BF16), v_ref[0, :kv, :], preferred_element_type=F32)
        o_ref[0, i * tq:(i + 1) * tq, :] = _subln(o, gain).astype(o_ref.dtype)
        if i + 1 < n_blocks:
            yield


N_ATTN_IN = 5


def _prompt_attn_decode_kernel(pt_ref, *refs, tq, n_pages):
    attn_in, rest = refs[:N_ATTN_IN], refs[N_ATTN_IN:]
    decode_in, rest = rest[:N_DECODE_IN], rest[N_DECODE_IN:]
    o_ref, od_ref, kbuf, vbuf, sem = rest
    step = pl.program_id(0) * pl.num_programs(1) + pl.program_id(1)
    n_steps = pl.num_programs(0) * pl.num_programs(1)
    chunk_steps = _decode_chunk_steps(pt_ref, *decode_in, od_ref, kbuf, vbuf, sem,
                                      step=step, n_steps=n_steps, n_pages=n_pages)
    n_blocks = attn_in[0].shape[1] // tq
    _run_interleaved(chunk_steps, _prompt_attn_pieces(*attn_in, o_ref, tq=tq),
                     [i + 1 for i in range(n_blocks)])


def _prompt_attn_with_decode(q, k, v, lamp, subln_g, decode_args, *, page_rows):
    b, s, _ = q.shape
    host = _decode_hosting(*decode_args, lamp, subln_g, n_steps=b * N_HEADS,
                           step_of=lambda bi, hi: bi * N_HEADS + hi, page_rows=page_rows)
    head = lambda w: pl.BlockSpec((1, s, w), lambda bi, hi, pt: (bi, 0, hi))
    n_pages = decode_args[5].shape[1]
    o, od = pl.pallas_call(
        functools.partial(_prompt_attn_decode_kernel, tq=ATTN_Q_TILE, n_pages=n_pages),
        grid_spec=pltpu.PrefetchScalarGridSpec(
            num_scalar_prefetch=1,
            grid=(b, N_HEADS),
            in_specs=[head(QK_DIM), head(QK_DIM), head(V_DIM), _const_spec(lamp.shape),
                      _const_spec(subln_g.shape)] + host["in_specs"],
            out_specs=[head(V_DIM), host["out_spec"]],
            scratch_shapes=host["scratch"],
        ),
        out_shape=[jax.ShapeDtypeStruct((b, s, N_HEADS * V_DIM), BF16), host["out_shape"]],
        compiler_params=pltpu.CompilerParams(
            dimension_semantics=("arbitrary", "arbitrary"), vmem_limit_bytes=VMEM_LIMIT_BYTES),
        name="prompt_attn_decode",
    )(host["prefetch"], q, k, v, lamp, subln_g, *host["operands"])
    return o, od.reshape(od.shape[0], N_HEADS * V_DIM)


MERGE_PIECE_WEIGHTS = (0.5, 1.0, 1.0, 1.4, 1.4, 1.4, 1.4, 1.4, 1.5)


def _merge_ffn_pieces(x_ref, ao_ref, u_ref, vs_ref, sga_ref, sgb_ref, ws_ref, bs_ref,
                      wao_ref, wso_ref, wo_ref, gffn_ref, wg_ref, wu_ref, wd_ref, gfin_ref,
                      y_ref, *, chunked):
    tile = x_ref.shape[0]
    sw = u_ref.shape[1]
    gd = sw // SG_GROUPS
    u = u_ref[...].astype(F32)
    if chunked:
        vs = vs_ref[...].astype(BF16)
        n_chunks = tile // CHUNK
        tri = (lax.broadcasted_iota(jnp.int32, (CHUNK, CHUNK), 0)
               >= lax.broadcasted_iota(jnp.int32, (CHUNK, CHUNK), 1))
        groups = []
        for g in range(SG_GROUPS):
            wg = jnp.where(tri, ws_ref[g], 0.0).astype(BF16)
            vg = jnp.concatenate([vs[c * CHUNK:(c + 1) * CHUNK, g * gd:(g + 1) * gd]
                                  for c in range(n_chunks)], axis=1)
            sg = jnp.dot(wg, vg, preferred_element_type=F32) + bs_ref[:, g:g + 1]
            groups.append(jnp.concatenate([sg[:, c * gd:(c + 1) * gd] for c in range(n_chunks)],
                                          axis=0))
        s = jnp.concatenate(groups, axis=1)
    else:
        s = vs_ref[...] * ws_ref[...] + bs_ref[...]
    so = (u * s).astype(BF16)
    yield
    attn = jnp.dot(ao_ref[...].astype(BF16), wao_ref[...], preferred_element_type=F32)
    gate = jnp.dot(so, wso_ref[...], preferred_element_type=F32)
    m = sga_ref[...].astype(F32) * attn + sgb_ref[...].astype(F32) * gate
    yield
    h = x_ref[...] + jnp.dot(m.astype(BF16), wo_ref[...], preferred_element_type=F32)
    hn = _rmsnorm(h, gffn_ref[...]).astype(BF16)
    yield
    d_ff = wg_ref.shape[1]
    step = d_ff // FFN_CHUNKS
    for c in range(FFN_CHUNKS):
        cols = slice(c * step, (c + 1) * step)
        a = jnp.dot(hn, wg_ref[:, cols], preferred_element_type=F32)
        yield
        b = jnp.dot(hn, wu_ref[:, cols], preferred_element_type=F32)
        f = (a * _sigmoid(a) * b).astype(BF16)
        yield
        h = h + jnp.dot(f, wd_ref[cols, :], preferred_element_type=F32)
        if c + 1 < FFN_CHUNKS:
            yield
    y_ref[...] = _rmsnorm(h, gfin_ref[...])


N_MERGE_IN = 16


def _merge_ffn_kernel(*refs, chunked):
    for _ in _merge_ffn_pieces(*refs, chunked=chunked):
        pass


def _merge_ffn_decode_kernel(pt_ref, *refs, n_pages):
    merge_in, rest = refs[:N_MERGE_IN], refs[N_MERGE_IN:]
    decode_in, rest = rest[:N_DECODE_IN], rest[N_DECODE_IN:]
    y_ref, o_ref, kbuf, vbuf, sem = rest
    chunk_steps = _decode_chunk_steps(pt_ref, *decode_in, o_ref, kbuf, vbuf, sem,
                                      step=pl.program_id(0), n_steps=pl.num_programs(0),
                                      n_pages=n_pages)
    assert len(MERGE_PIECE_WEIGHTS) == 3 + 3 * FFN_CHUNKS
    _run_interleaved(chunk_steps, _merge_ffn_pieces(*merge_in, y_ref, chunked=True),
                     MERGE_PIECE_WEIGHTS)


def _merge_ffn(x, ao, u, vs, sga, sgb, ws, bs, wao, wso, wo, gffn, wg, wu, wd, gfin, *, tile):
    n, dm = x.shape
    assert n % tile == 0 and wg.shape[1] % (FFN_CHUNKS * LANES) == 0
    row = lambda a: pl.BlockSpec((tile, a.shape[1]), lambda i: (i, 0))
    consts = (ws, bs, wao, wso, wo, gffn, wg, wu, wd, gfin)
    return pl.pallas_call(
        functools.partial(_merge_ffn_kernel, chunked=False),
        grid=(n // tile,),
        in_specs=[row(a) for a in (x, ao, u, vs, sga, sgb)] + [_const_spec(a.shape) for a in consts],
        out_specs=pl.BlockSpec((tile, dm), lambda i: (i, 0)),
        out_shape=jax.ShapeDtypeStruct((n, dm), F32),
        compiler_params=pltpu.CompilerParams(
            dimension_semantics=("arbitrary",), vmem_limit_bytes=VMEM_LIMIT_BYTES),
        name="merge_ffn_rows",
    )(x, ao, u, vs, sga, sgb, *consts)


def _merge_ffn_with_decode(x, ao, u, vs, sga, sgb, ws, bs, wao, wso, wo, gffn, wg, wu, wd, gfin,
                           lamp, subln_g, decode_args, *, tile, page_rows):
    n, dm = x.shape
    n_steps = n // tile
    assert n % tile == 0 and tile % CHUNK == 0 and wg.shape[1] % (FFN_CHUNKS * LANES) == 0
    host = _decode_hosting(*decode_args, lamp, subln_g, n_steps=n_steps, step_of=lambda i: i,
                           page_rows=page_rows)
    row = lambda a: pl.BlockSpec((tile, a.shape[1]), lambda i, pt: (i, 0))
    consts = (ws, bs, wao, wso, wo, gffn, wg, wu, wd, gfin)
    n_pages = decode_args[5].shape[1]
    y, o = pl.pallas_call(
        functools.partial(_merge_ffn_decode_kernel, n_pages=n_pages),
        grid_spec=pltpu.PrefetchScalarGridSpec(
            num_scalar_prefetch=1,
            grid=(n_steps,),
            in_specs=([row(a) for a in (x, ao, u, vs, sga, sgb)]
                      + [_const_spec(a.shape) for a in consts] + host["in_specs"]),
            out_specs=[pl.BlockSpec((tile, dm), lambda i, pt: (i, 0)), host["out_spec"]],
            scratch_shapes=host["scratch"],
        ),
        out_shape=[jax.ShapeDtypeStruct((n, dm), F32), host["out_shape"]],
        compiler_params=pltpu.CompilerParams(
            dimension_semantics=("arbitrary",), vmem_limit_bytes=VMEM_LIMIT_BYTES),
        name="merge_ffn_decode",
    )(host["prefetch"], x, ao, u, vs, sga, sgb, *consts, *host["operands"])
    return y, o.reshape(o.shape[0], N_HEADS * V_DIM)


def kernel(x_prompt, x_sample, cache_k, cache_v, page_table, w_in, lam_q1, lam_k1, lam_q2, lam_k2,
           subln_g, sg_ln_g, sg_ln_b, w_spatial, b_spatial, w_attn_out, w_sg_out, w_o,
           norm_mix_g, norm_ffn_g, w_gate, w_up, w_down, norm_final_g):
    assert w_in.shape[0] == 1 and cache_k.shape[0] == 1, "single-layer trunk"
    b, s, dm = x_prompt.shape
    bd, t_s, _ = x_sample.shape
    assert t_s == 1 and s % CHUNK == 0
    page = cache_k.shape[2]
    assert cache_k.shape[3:] == (N_HEADS, QK_DIM) and cache_v.shape[3:] == (N_HEADS, V_DIM)
    qw, aw = N_HEADS * QK_DIM, N_HEADS * V_DIM
    sw = sg_ln_g.shape[-1]
    gd = sw // SG_GROUPS
    page_rows = page * N_HEADS

    row1 = lambda a: a.reshape(1, -1)
    win = w_in[0].astype(BF16)
    gmix, lng, lnb = row1(norm_mix_g[0]), row1(sg_ln_g[0]), row1(sg_ln_b[0])
    lamp = jnp.stack([lam_q1[0], lam_k1[0], lam_q2[0], lam_k2[0]])
    subg = row1(subln_g[0])
    merge_w = (w_attn_out[0].astype(BF16), w_sg_out[0].astype(BF16), w_o[0].astype(BF16),
               row1(norm_ffn_g[0]), w_gate[0].astype(BF16), w_up[0].astype(BF16),
               w_down[0].astype(BF16), row1(norm_final_g))

    xp = x_prompt.reshape(b * s, dm)
    xs = x_sample.reshape(bd, dm)

    qs, ks, vs_, us, vss, sgas, sgbs = _inproj(
        xs, gmix, win, lng, lnb, tile=bd, emit_bf16_kv=False)
    qp, kp, vp, kbp, vbp, up, vsp, sgap, sgbp = _inproj(
        xp, gmix, win, lng, lnb, tile=ROW_TILE, emit_bf16_kv=True)

    ck, cv = cache_k.reshape(-1, QK_DIM), cache_v.reshape(-1, V_DIM)
    n_first = int(bd * ATTN_HOSTED_REQUEST_SHARE)
    decode_args = lambda lo, hi: (qs[lo:hi], ks[lo * N_HEADS:hi * N_HEADS],
                                  vs_[lo * N_HEADS:hi * N_HEADS], ck, cv, page_table[lo:hi])

    ao_p, ao_s0 = _prompt_attn_with_decode(
        qp.reshape(b, s, qw), kbp.reshape(b, s, qw), vbp.reshape(b, s, aw), lamp, subg,
        decode_args(0, n_first), page_rows=page_rows)
    yp, ao_s1 = _merge_ffn_with_decode(
        xp, ao_p.reshape(b * s, aw), up, vsp, sgap, sgbp,
        w_spatial[0], jnp.transpose(b_spatial[0]), *merge_w, lamp, subg,
        decode_args(n_first, bd), tile=ROW_TILE, page_rows=page_rows)
    ao_s = jnp.concatenate([ao_s0, ao_s1], axis=0)
    ys = _merge_ffn(xs, ao_s, us, vss, sgas, sgbs,
                    row1(jnp.repeat(w_spatial[0, :, 0, 0], gd)), row1(jnp.repeat(b_spatial[0, :, 0], gd)),
                    *merge_w, tile=bd)

    y_prompt = yp.reshape(b, s, dm)
    y_sample = ys.reshape(bd, 1, dm)
    k_prompt = kp.reshape(1, b, s, N_HEADS, QK_DIM)
    v_prompt = vp.reshape(1, b, s, N_HEADS, V_DIM)
    k_sample = ks.reshape(1, bd, 1, N_HEADS, QK_DIM)
    v_sample = vs_.reshape(1, bd, 1, N_HEADS, V_DIM)
    sg_v_prompt = vsp.reshape(b, s, sw)[:, s - CHUNK:, :].reshape(1, b, CHUNK, SG_GROUPS, gd)
    sg_v_sample = vss.reshape(1, bd, 1, SG_GROUPS, gd)
    return (y_prompt, y_sample, k_prompt, v_prompt, k_sample, v_sample, sg_v_prompt, sg_v_sample)
```

```python
import functools
import math

import jax
import jax.numpy as jnp
from jax import lax
from jax.experimental import pallas as pl
from jax.experimental.pallas import tpu as pltpu

F32 = jnp.float32
BF16 = jnp.bfloat16

N_HEADS = 4
HEAD_DIM = 64
QK_DIM = 2 * HEAD_DIM
V_DIM = 2 * HEAD_DIM
N_SCORE_ROWS = 2 * N_HEADS
SG_GROUPS = 4
CHUNK = 128
EPS = 1e-6
QK_SCALE = HEAD_DIM ** -0.5
LAM_INIT = 0.8 - 0.6 * math.exp(-0.3 * 0)

LANES = 128
VMEM_LIMIT_BYTES = 60 * 1024 * 1024

ROW_TILE = 256
ATTN_Q_TILE = 256
PAGES_PER_CHUNK = 16
KV_SLOTS = 2
FFN_CHUNKS = 2
ATTN_HOSTED_REQUEST_SHARE = 0.5


def _const_spec(shape):
    zeros = (0,) * len(shape)
    return pl.BlockSpec(shape, lambda *_: zeros, pipeline_mode=pl.Buffered(1))


def _rmsnorm(x, g):
    return x * lax.rsqrt(jnp.mean(x * x, axis=-1, keepdims=True) + EPS) * g


def _gelu(x):
    return 0.5 * x * (1.0 + lax.erf(x * math.sqrt(0.5)))


def _sigmoid(x):
    return 1.0 / (1.0 + jnp.exp(-x))


def _lam(lamp_ref):
    p = lamp_ref[...]
    a = jnp.sum(p[0:1] * p[1:2], axis=-1, keepdims=True)
    b = jnp.sum(p[2:3] * p[3:4], axis=-1, keepdims=True)
    return jnp.exp(a) - jnp.exp(b) + LAM_INIT


def _subln(o, gain):
    return o * lax.rsqrt(jnp.mean(o * o, axis=-1, keepdims=True) + EPS) * gain


def _inproj_kernel(x_ref, gmix_ref, win_ref, lng_ref, lnb_ref, *out_refs, widths, emit_bf16_kv):
    qw, aw, sw, dm = widths
    if emit_bf16_kv:
        q_ref, k_ref, v_ref, kb_ref, vb_ref, u_ref, vs_ref, sga_ref, sgb_ref = out_refs
    else:
        q_ref, k_ref, v_ref, u_ref, vs_ref, sga_ref, sgb_ref = out_refs
    tile = x_ref.shape[0]
    xn = _rmsnorm(x_ref[...], gmix_ref[...]).astype(BF16)

    def proj(lo, width):
        return jnp.dot(xn, win_ref[:, lo:lo + width], preferred_element_type=F32)

    def store_by_head(ref, val, head_dim):
        for h in range(N_HEADS):
            ref[pl.ds(h, tile, stride=N_HEADS), :] = val[:, h * head_dim:(h + 1) * head_dim]

    q_ref[...] = (proj(0, qw) * QK_SCALE).astype(BF16)
    k = proj(qw, qw)
    store_by_head(k_ref, k, QK_DIM)
    v = proj(2 * qw, aw)
    store_by_head(v_ref, v, V_DIM)
    if emit_bf16_kv:
        kb_ref[...] = k.astype(BF16)
        vb_ref[...] = v.astype(BF16)
    off = 2 * qw + aw
    u_ref[...] = _gelu(proj(off, sw)).astype(BF16)
    gv = _gelu(proj(off + sw, sw))
    mu = jnp.mean(gv, axis=-1, keepdims=True)
    gc = gv - mu
    var = jnp.mean(gc * gc, axis=-1, keepdims=True)
    vs_ref[...] = gc * lax.rsqrt(var + EPS) * lng_ref[...] + lnb_ref[...]
    off += 2 * sw
    sga_ref[...] = _sigmoid(proj(off, dm)).astype(BF16)
    sgb_ref[...] = _sigmoid(proj(off + dm, dm)).astype(BF16)


def _inproj(x, gmix, win, lng, lnb, *, tile, emit_bf16_kv):
    n, dm = x.shape
    sw = lng.shape[-1]
    qw = N_HEADS * QK_DIM
    aw = N_HEADS * V_DIM
    assert win.shape == (dm, 2 * qw + aw + 2 * sw + 2 * dm)
    assert n % tile == 0
    row = lambda w: pl.BlockSpec((tile, w), lambda i: (i, 0))
    by_head = lambda w: pl.BlockSpec((tile * N_HEADS, w), lambda i: (i, 0))
    outs = [(n, qw, BF16, row(qw)),
            (n * N_HEADS, QK_DIM, F32, by_head(QK_DIM)),
            (n * N_HEADS, V_DIM, F32, by_head(V_DIM))]
    if emit_bf16_kv:
        outs += [(n, qw, BF16, row(qw)), (n, aw, BF16, row(aw))]
    outs += [(n, sw, BF16, row(sw)), (n, sw, F32, row(sw)),
             (n, dm, BF16, row(dm)), (n, dm, BF16, row(dm))]
    return pl.pallas_call(
        functools.partial(_inproj_kernel, widths=(qw, aw, sw, dm), emit_bf16_kv=emit_bf16_kv),
        grid=(n // tile,),
        in_specs=[row(dm), _const_spec((1, dm)), _const_spec(win.shape),
                  _const_spec((1, sw)), _const_spec((1, sw))],
        out_specs=[o[3] for o in outs],
        out_shape=[jax.ShapeDtypeStruct((o[0], o[1]), o[2]) for o in outs],
        compiler_params=pltpu.CompilerParams(
            dimension_semantics=("arbitrary",), vmem_limit_bytes=VMEM_LIMIT_BYTES),
        name="inproj",
    )(x, gmix, win, lng, lnb)


N_DECODE_IN = 7


def _decode_chunk_steps(pt_ref, q_ref, kn_ref, vn_ref, lamp_ref, sg_ref, ck_hbm, cv_hbm,
                        o_ref, kbuf, vbuf, sem, *, step, n_steps, n_pages):
    rps, n_rows, _ = q_ref.shape
    chunk_rows = kbuf.shape[1]
    page_rows = chunk_rows // PAGES_PER_CHUNK
    n_chunks = n_pages // PAGES_PER_CHUNK
    per_step = rps * n_chunks
    assert per_step % KV_SLOTS == 0

    def page_copies(r, c, slot, p):
        idx = pt_ref[r * n_pages + c * PAGES_PER_CHUNK + p]
        src = pl.ds(pl.multiple_of(idx * page_rows, page_rows), page_rows)
        dst = pl.ds(p * page_rows, page_rows)
        return (pltpu.make_async_copy(ck_hbm.at[src], kbuf.at[slot, dst], sem.at[slot, 0]),
                pltpu.make_async_copy(cv_hbm.at[src], vbuf.at[slot, dst], sem.at[slot, 1]))

    def start_chunk(r, c, slot):
        for p in range(PAGES_PER_CHUNK):
            for cp in page_copies(r, c, slot, p):
                cp.start()

    def wait_chunk(r, c, slot):
        for p in range(PAGES_PER_CHUNK):
            for cp in page_copies(r, c, slot, p):
                cp.wait()

    lam = _lam(lamp_ref)
    gain = sg_ref[...] * (1.0 - LAM_INIT)
    own_map = (lax.broadcasted_iota(jnp.int32, (n_rows, QK_DIM), 1) // HEAD_DIM
               == lax.broadcasted_iota(jnp.int32, (n_rows, QK_DIM), 0) % 2)
    own_head = (lax.broadcasted_iota(jnp.int32, (n_rows, chunk_rows), 1) % N_HEADS
                == lax.broadcasted_iota(jnp.int32, (n_rows, chunk_rows), 0) // 2)

    def fetch(g):
        nxt, g_local = divmod(g, per_step)
        j, c = divmod(g_local, n_chunks)
        if nxt == 0:
            start_chunk(step * rps + j, c, g_local % KV_SLOTS)
        else:
            @pl.when(step + nxt < n_steps)
            def _():
                start_chunk((step + nxt) * rps + j, c, g_local % KV_SLOTS)

    assert KV_SLOTS <= per_step

    @pl.when(step == 0)
    def _():
        for g in range(KV_SLOTS):
            fetch(g)

    state = {}

    def chunk_step(g):
        j, c = divmod(g, n_chunks)
        slot = g % KV_SLOTS
        r = step * rps + j
        if c == 0:
            qb = jnp.where(own_map, q_ref[j], 0.0).astype(BF16)
            kn = kn_ref[j].astype(BF16).astype(F32)
            m = jnp.sum(qb.astype(F32) * kn, axis=-1, keepdims=True)
            state.update(qb=qb, m=m, l=jnp.ones_like(m),
                         acc=vn_ref[j].astype(BF16).astype(F32))
        wait_chunk(r, c, slot)
        kb = kbuf[slot].astype(BF16)
        s = lax.dot_general(state["qb"], kb, (((1,), (1,)), ((), ())), preferred_element_type=F32)
        s = jnp.where(own_head, s, -jnp.inf)
        m_new = jnp.maximum(state["m"], jnp.max(s, axis=-1, keepdims=True))
        alpha = jnp.exp(state["m"] - m_new)
        p = jnp.exp(s - m_new)
        state["l"] = alpha * state["l"] + jnp.sum(p, axis=-1, keepdims=True)
        state["acc"] = alpha * state["acc"] + jnp.dot(
            p.astype(BF16), vbuf[slot].astype(BF16), preferred_element_type=F32)
        state["m"] = m_new
        fetch(g + KV_SLOTS)
        if c == n_chunks - 1:
            a = state["acc"] * (1.0 / state["l"])
            for h in range(N_HEADS):
                o = a[2 * h:2 * h + 1] - lam * a[2 * h + 1:2 * h + 2]
                o_ref[j, h:h + 1, :] = _subln(o, gain)

    return [functools.partial(chunk_step, g) for g in range(per_step)]


def _run_interleaved(chunk_steps, pieces, weights):
    total, n, k, start = sum(weights), len(chunk_steps), 0, 0
    for w in weights:
        while k < n and start * n >= k * total:
            chunk_steps[k]()
            k += 1
        next(pieces, None)
        start += w
    for chunk_step in chunk_steps[k:]:
        chunk_step()
    for _ in pieces:
        raise AssertionError("more pieces than weights")


def _decode_hosting(q, k_new, v_new, cache_k, cache_v, page_table, lamp, subln_g,
                    *, n_steps, step_of, page_rows):
    n_req, n_pages = page_table.shape
    assert n_req % n_steps == 0 and n_pages % PAGES_PER_CHUNK == 0
    rps = n_req // n_steps
    per_map = lambda a: jnp.repeat(a.astype(F32).reshape(n_req, N_HEADS, QK_DIM), 2, axis=1)
    req = lambda rows, w: pl.BlockSpec((rps, rows, w), lambda *a: (step_of(*a[:-1]), 0, 0))
    hbm = pl.BlockSpec(memory_space=pl.ANY)
    chunk_rows = PAGES_PER_CHUNK * page_rows
    return dict(
        prefetch=page_table.reshape(-1),
        in_specs=[req(N_SCORE_ROWS, QK_DIM), req(N_SCORE_ROWS, QK_DIM), req(N_SCORE_ROWS, V_DIM),
                  _const_spec(lamp.shape), _const_spec(subln_g.shape), hbm, hbm],
        operands=[per_map(q), per_map(k_new), per_map(v_new), lamp, subln_g, cache_k, cache_v],
        out_spec=req(N_HEADS, V_DIM),
        out_shape=jax.ShapeDtypeStruct((n_req, N_HEADS, V_DIM), F32),
        scratch=[pltpu.VMEM((KV_SLOTS, chunk_rows, QK_DIM), cache_k.dtype),
                 pltpu.VMEM((KV_SLOTS, chunk_rows, V_DIM), cache_v.dtype),
                 pltpu.SemaphoreType.DMA((KV_SLOTS, 2))],
    )


def _prompt_attn_pieces(q_ref, k_ref, v_ref, lamp_ref, sg_ref, o_ref, *, tq):
    seq = q_ref.shape[1]
    lam = _lam(lamp_ref)
    gain = sg_ref[...] * (1.0 - LAM_INIT)
    lane = lax.broadcasted_iota(jnp.int32, (tq, QK_DIM), 1)
    first_map = lane < HEAD_DIM
    row = lax.broadcasted_iota(jnp.int32, (2 * tq, tq), 0)
    col = lax.broadcasted_iota(jnp.int32, (2 * tq, tq), 1)
    causal = col <= jnp.where(row >= tq, row - tq, row)
    n_blocks = seq // tq
    for i in range(n_blocks):
        kv = (i + 1) * tq
        q = q_ref[0, i * tq:(i + 1) * tq, :]
        zero = jnp.zeros_like(q)
        qq = jnp.concatenate([jnp.where(first_map, q, zero), jnp.where(first_map, zero, q)], axis=0)
        s = lax.dot_general(qq, k_ref[0, :kv, :], (((1,), (1,)), ((), ())),
                            preferred_element_type=F32)
        s_diag = jnp.where(causal, s[:, kv - tq:], -jnp.inf)
        s = s_diag if i == 0 else jnp.concatenate([s[:, :kv - tq], s_diag], axis=1)
        m = jnp.max(s, axis=-1, keepdims=True)
        p = jnp.exp(s - m)
        r = 1.0 / jnp.sum(p, axis=-1, keepdims=True)
        w = p[:tq] * r[:tq] - lam * (p[tq:] * r[tq:])
        o = jnp.dot(w.astype(BF16), v_ref[0, :kv, :], preferred_element_type=F32)
        o_ref[0, i * tq:(i + 1) * tq, :] = _subln(o, gain).astype(o_ref.dtype)
        if i + 1 < n_blocks:
            yield


N_ATTN_IN = 5


def _prompt_attn_decode_kernel(pt_ref, *refs, tq, n_pages):
    attn_in, rest = refs[:N_ATTN_IN], refs[N_ATTN_IN:]
    decode_in, rest = rest[:N_DECODE_IN], rest[N_DECODE_IN:]
    o_ref, od_ref, kbuf, vbuf, sem = rest
    step = pl.program_id(0) * pl.num_programs(1) + pl.program_id(1)
    n_steps = pl.num_programs(0) * pl.num_programs(1)
    chunk_steps = _decode_chunk_steps(pt_ref, *decode_in, od_ref, kbuf, vbuf, sem,
                                      step=step, n_steps=n_steps, n_pages=n_pages)
    n_blocks = attn_in[0].shape[1] // tq
    _run_interleaved(chunk_steps, _prompt_attn_pieces(*attn_in, o_ref, tq=tq),
                     [i + 1 for i in range(n_blocks)])


def _prompt_attn_with_decode(q, k, v, lamp, subln_g, decode_args, *, page_rows):
    b, s, _ = q.shape
    host = _decode_hosting(*decode_args, lamp, subln_g, n_steps=b * N_HEADS,
                           step_of=lambda bi, hi: bi * N_HEADS + hi, page_rows=page_rows)
    head = lambda w: pl.BlockSpec((1, s, w), lambda bi, hi, pt: (bi, 0, hi))
    n_pages = decode_args[5].shape[1]
    o, od = pl.pallas_call(
        functools.partial(_prompt_attn_decode_kernel, tq=ATTN_Q_TILE, n_pages=n_pages),
        grid_spec=pltpu.PrefetchScalarGridSpec(
            num_scalar_prefetch=1,
            grid=(b, N_HEADS),
            in_specs=[head(QK_DIM), head(QK_DIM), head(V_DIM), _const_spec(lamp.shape),
                      _const_spec(subln_g.shape)] + host["in_specs"],
            out_specs=[head(V_DIM), host["out_spec"]],
            scratch_shapes=host["scratch"],
        ),
        out_shape=[jax.ShapeDtypeStruct((b, s, N_HEADS * V_DIM), BF16), host["out_shape"]],
        compiler_params=pltpu.CompilerParams(
            dimension_semantics=("arbitrary", "arbitrary"), vmem_limit_bytes=VMEM_LIMIT_BYTES),
        name="prompt_attn_decode",
    )(host["prefetch"], q, k, v, lamp, subln_g, *host["operands"])
    return o, od.reshape(od.shape[0], N_HEADS * V_DIM)


MERGE_PIECE_WEIGHTS = (0.5, 1.0, 1.0, 1.4, 1.4, 1.4, 1.4, 1.4, 1.5)


def _merge_ffn_pieces(x_ref, ao_ref, u_ref, vs_ref, sga_ref, sgb_ref, ws_ref, bs_ref,
                      wao_ref, wso_ref, wo_ref, gffn_ref, wg_ref, wu_ref, wd_ref, gfin_ref,
                      y_ref, *, chunked):
    tile = x_ref.shape[0]
    sw = u_ref.shape[1]
    gd = sw // SG_GROUPS
    u = u_ref[...].astype(F32)
    if chunked:
        vs = vs_ref[...].astype(BF16)
        n_chunks = tile // CHUNK
        tri = (lax.broadcasted_iota(jnp.int32, (CHUNK, CHUNK), 0)
               >= lax.broadcasted_iota(jnp.int32, (CHUNK, CHUNK), 1))
        groups = []
        for g in range(SG_GROUPS):
            wg = jnp.where(tri, ws_ref[g], 0.0).astype(BF16)
            vg = jnp.concatenate([vs[c * CHUNK:(c + 1) * CHUNK, g * gd:(g + 1) * gd]
                                  for c in range(n_chunks)], axis=1)
            sg = jnp.dot(wg, vg, preferred_element_type=F32) + bs_ref[:, g:g + 1]
            groups.append(jnp.concatenate([sg[:, c * gd:(c + 1) * gd] for c in range(n_chunks)],
                                          axis=0))
        s = jnp.concatenate(groups, axis=1)
    else:
        s = vs_ref[...] * ws_ref[...] + bs_ref[...]
    so = (u * s).astype(BF16)
    yield
    attn = jnp.dot(ao_ref[...].astype(BF16), wao_ref[...], preferred_element_type=F32)
    gate = jnp.dot(so, wso_ref[...], preferred_element_type=F32)
    m = sga_ref[...].astype(F32) * attn + sgb_ref[...].astype(F32) * gate
    yield
    h = x_ref[...] + jnp.dot(m.astype(BF16), wo_ref[...], preferred_element_type=F32)
    hn = _rmsnorm(h, gffn_ref[...]).astype(BF16)
    yield
    d_ff = wg_ref.shape[1]
    step = d_ff // FFN_CHUNKS
    for c in range(FFN_CHUNKS):
        cols = slice(c * step, (c + 1) * step)
        a = jnp.dot(hn, wg_ref[:, cols], preferred_element_type=F32)
        yield
        b = jnp.dot(hn, wu_ref[:, cols], preferred_element_type=F32)
        f = (a * _sigmoid(a) * b).astype(BF16)
        yield
        h = h + jnp.dot(f, wd_ref[cols, :], preferred_element_type=F32)
        if c + 1 < FFN_CHUNKS:
            yield
    y_ref[...] = _rmsnorm(h, gfin_ref[...])


N_MERGE_IN = 16


def _merge_ffn_kernel(*refs, chunked):
    for _ in _merge_ffn_pieces(*refs, chunked=chunked):
        pass


def _merge_ffn_decode_kernel(pt_ref, *refs, n_pages):
    merge_in, rest = refs[:N_MERGE_IN], refs[N_MERGE_IN:]
    decode_in, rest = rest[:N_DECODE_IN], rest[N_DECODE_IN:]
    y_ref, o_ref, kbuf, vbuf, sem = rest
    chunk_steps = _decode_chunk_steps(pt_ref, *decode_in, o_ref, kbuf, vbuf, sem,
                                      step=pl.program_id(0), n_steps=pl.num_programs(0),
                                      n_pages=n_pages)
    assert len(MERGE_PIECE_WEIGHTS) == 3 + 3 * FFN_CHUNKS
    _run_interleaved(chunk_steps, _merge_ffn_pieces(*merge_in, y_ref, chunked=True),
                     MERGE_PIECE_WEIGHTS)


def _merge_ffn(x, ao, u, vs, sga, sgb, ws, bs, wao, wso, wo, gffn, wg, wu, wd, gfin, *, tile):
    n, dm = x.shape
    assert n % tile == 0 and wg.shape[1] % (FFN_CHUNKS * LANES) == 0
    row = lambda a: pl.BlockSpec((tile, a.shape[1]), lambda i: (i, 0))
    consts = (ws, bs, wao, wso, wo, gffn, wg, wu, wd, gfin)
    return pl.pallas_call(
        functools.partial(_merge_ffn_kernel, chunked=False),
        grid=(n // tile,),
        in_specs=[row(a) for a in (x, ao, u, vs, sga, sgb)] + [_const_spec(a.shape) for a in consts],
        out_specs=pl.BlockSpec((tile, dm), lambda i: (i, 0)),
        out_shape=jax.ShapeDtypeStruct((n, dm), F32),
        compiler_params=pltpu.CompilerParams(
            dimension_semantics=("arbitrary",), vmem_limit_bytes=VMEM_LIMIT_BYTES),
        name="merge_ffn_rows",
    )(x, ao, u, vs, sga, sgb, *consts)


def _merge_ffn_with_decode(x, ao, u, vs, sga, sgb, ws, bs, wao, wso, wo, gffn, wg, wu, wd, gfin,
                           lamp, subln_g, decode_args, *, tile, page_rows):
    n, dm = x.shape
    n_steps = n // tile
    assert n % tile == 0 and tile % CHUNK == 0 and wg.shape[1] % (FFN_CHUNKS * LANES) == 0
    host = _decode_hosting(*decode_args, lamp, subln_g, n_steps=n_steps, step_of=lambda i: i,
                           page_rows=page_rows)
    row = lambda a: pl.BlockSpec((tile, a.shape[1]), lambda i, pt: (i, 0))
    consts = (ws, bs, wao, wso, wo, gffn, wg, wu, wd, gfin)
    n_pages = decode_args[5].shape[1]
    y, o = pl.pallas_call(
        functools.partial(_merge_ffn_decode_kernel, n_pages=n_pages),
        grid_spec=pltpu.PrefetchScalarGridSpec(
            num_scalar_prefetch=1,
            grid=(n_steps,),
            in_specs=([row(a) for a in (x, ao, u, vs, sga, sgb)]
                      + [_const_spec(a.shape) for a in consts] + host["in_specs"]),
            out_specs=[pl.BlockSpec((tile, dm), lambda i, pt: (i, 0)), host["out_spec"]],
            scratch_shapes=host["scratch"],
        ),
        out_shape=[jax.ShapeDtypeStruct((n, dm), F32), host["out_shape"]],
        compiler_params=pltpu.CompilerParams(
            dimension_semantics=("arbitrary",), vmem_limit_bytes=VMEM_LIMIT_BYTES),
        name="merge_ffn_decode",
    )(host["prefetch"], x, ao, u, vs, sga, sgb, *consts, *host["operands"])
    return y, o.reshape(o.shape[0], N_HEADS * V_DIM)


def kernel(x_prompt, x_sample, cache_k, cache_v, page_table, w_in, lam_q1, lam_k1, lam_q2, lam_k2,
           subln_g, sg_ln_g, sg_ln_b, w_spatial, b_spatial, w_attn_out, w_sg_out, w_o,
           norm_mix_g, norm_ffn_g, w_gate, w_up, w_down, norm_final_g):
    assert w_in.shape[0] == 1 and cache_k.shape[0] == 1, "single-layer trunk"
    b, s, dm = x_prompt.shape
    bd, t_s, _ = x_sample.shape
    assert t_s == 1 and s % CHUNK == 0
    page = cache_k.shape[2]
    assert cache_k.shape[3:] == (N_HEADS, QK_DIM) and cache_v.shape[3:] == (N_HEADS, V_DIM)
    qw, aw = N_HEADS * QK_DIM, N_HEADS * V_DIM
    sw = sg_ln_g.shape[-1]
    gd = sw // SG_GROUPS
    page_rows = page * N_HEADS

    row1 = lambda a: a.reshape(1, -1)
    win = w_in[0].astype(BF16)
    gmix, lng, lnb = row1(norm_mix_g[0]), row1(sg_ln_g[0]), row1(sg_ln_b[0])
    lamp = jnp.stack([lam_q1[0], lam_k1[0], lam_q2[0], lam_k2[0]])
    subg = row1(subln_g[0])
    merge_w = (w_attn_out[0].astype(BF16), w_sg_out[0].astype(BF16), w_o[0].astype(BF16),
               row1(norm_ffn_g[0]), w_gate[0].astype(BF16), w_up[0].astype(BF16),
               w_down[0].astype(BF16), row1(norm_final_g))

    xp = x_prompt.reshape(b * s, dm)
    xs = x_sample.reshape(bd, dm)

    qs, ks, vs_, us, vss, sgas, sgbs = _inproj(
        xs, gmix, win, lng, lnb, tile=bd, emit_bf16_kv=False)
    qp, kp, vp, kbp, vbp, up, vsp, sgap, sgbp = _inproj(
        xp, gmix, win, lng, lnb, tile=ROW_TILE, emit_bf16_kv=True)

    ck, cv = cache_k.reshape(-1, QK_DIM), cache_v.reshape(-1, V_DIM)
    n_first = int(bd * ATTN_HOSTED_REQUEST_SHARE)
    decode_args = lambda lo, hi: (qs[lo:hi], ks[lo * N_HEADS:hi * N_HEADS],
                                  vs_[lo * N_HEADS:hi * N_HEADS], ck, cv, page_table[lo:hi])

    ao_p, ao_s0 = _prompt_attn_with_decode(
        qp.reshape(b, s, qw), kbp.reshape(b, s, qw), vbp.reshape(b, s, aw), lamp, subg,
        decode_args(0, n_first), page_rows=page_rows)
    yp, ao_s1 = _merge_ffn_with_decode(
        xp, ao_p.reshape(b * s, aw), up, vsp, sgap, sgbp,
        w_spatial[0], jnp.transpose(b_spatial[0]), *merge_w, lamp, subg,
        decode_args(n_first, bd), tile=ROW_TILE, page_rows=page_rows)
    ao_s = jnp.concatenate([ao_s0, ao_s1], axis=0)
    ys = _merge_ffn(xs, ao_s, us, vss, sgas, sgbs,
                    row1(jnp.repeat(w_spatial[0, :, 0, 0], gd)), row1(jnp.repeat(b_spatial[0, :, 0], gd)),
                    *merge_w, tile=bd)

    y_prompt = yp.reshape(b, s, dm)
    y_sample = ys.reshape(bd, 1, dm)
    k_prompt = kp.reshape(1, b, s, N_HEADS, QK_DIM)
    v_prompt = vp.reshape(1, b, s, N_HEADS, V_DIM)
    k_sample = ks.reshape(1, bd, 1, N_HEADS, QK_DIM)
    v_sample = vs_.reshape(1, bd, 1, N_HEADS, V_DIM)
    sg_v_prompt = vsp.reshape(b, s, sw)[:, s - CHUNK:, :].reshape(1, b, CHUNK, SG_GROUPS, gd)
    sg_v_sample = vss.reshape(1, bd, 1, SG_GROUPS, gd)
    return (y_prompt, y_sample, k_prompt, v_prompt, k_sample, v_sample, sg_v_prompt, sg_v_sample)
```

```python
import functools
import math

import jax
import jax.numpy as jnp
from jax import lax
from jax.experimental import pallas as pl
from jax.experimental.pallas import tpu as pltpu

F32 = jnp.float32
BF16 = jnp.bfloat16

N_HEADS = 4
HEAD_DIM = 64
QK_DIM = 2 * HEAD_DIM
V_DIM = 2 * HEAD_DIM
N_SCORE_ROWS = 2 * N_HEADS
SG_GROUPS = 4
CHUNK = 128
EPS = 1e-6
QK_SCALE = HEAD_DIM ** -0.5
LAM_INIT = 0.8 - 0.6 * math.exp(-0.3 * 0)

LANES = 128
VMEM_LIMIT_BYTES = 60 * 1024 * 1024

ROW_TILE = 256
ATTN_Q_TILE = 256
PAGES_PER_CHUNK = 16
KV_SLOTS = 2
FFN_CHUNKS = 2
ATTN_HOSTED_REQUEST_SHARE = 0.5


def _const_spec(shape):
    zeros = (0,) * len(shape)
    return pl.BlockSpec(shape, lambda *_: zeros, pipeline_mode=pl.Buffered(1))


def _rmsnorm(x, g):
    return x * lax.rsqrt(jnp.mean(x * x, axis=-1, keepdims=True) + EPS) * g


def _gelu(x):
    return 0.5 * x * (1.0 + lax.erf(x * math.sqrt(0.5)))


def _sigmoid(x):
    return 1.0 / (1.0 + jnp.exp(-x))


def _lam(lamp_ref):
    p = lamp_ref[...]
    a = jnp.sum(p[0:1] * p[1:2], axis=-1, keepdims=True)
    b = jnp.sum(p[2:3] * p[3:4], axis=-1, keepdims=True)
    return jnp.exp(a) - jnp.exp(b) + LAM_INIT


def _subln(o, gain):
    return o * lax.rsqrt(jnp.mean(o * o, axis=-1, keepdims=True) + EPS) * gain


def _inproj_kernel(x_ref, gmix_ref, win_ref, lng_ref, lnb_ref, *out_refs, widths, emit_bf16_kv):
    qw, aw, sw, dm = widths
    if emit_bf16_kv:
        q_ref, k_ref, v_ref, kb_ref, vb_ref, u_ref, vs_ref, sga_ref, sgb_ref = out_refs
    else:
        q_ref, k_ref, v_ref, u_ref, vs_ref, sga_ref, sgb_ref = out_refs
    tile = x_ref.shape[0]
    xn = _rmsnorm(x_ref[...], gmix_ref[...]).astype(BF16)

    def proj(lo, width):
        return jnp.dot(xn, win_ref[:, lo:lo + width], preferred_element_type=F32)

    def store_by_head(ref, val, head_dim):
        for h in range(N_HEADS):
            ref[pl.ds(h, tile, stride=N_HEADS), :] = val[:, h * head_dim:(h + 1) * head_dim]

    q_ref[...] = (proj(0, qw) * QK_SCALE).astype(BF16)
    k = proj(qw, qw)
    store_by_head(k_ref, k, QK_DIM)
    v = proj(2 * qw, aw)
    store_by_head(v_ref, v, V_DIM)
    if emit_bf16_kv:
        kb_ref[...] = k.astype(BF16)
        vb_ref[...] = v.astype(BF16)
    off = 2 * qw + aw
    u_ref[...] = _gelu(proj(off, sw)).astype(BF16)
    gv = _gelu(proj(off + sw, sw))
    mu = jnp.mean(gv, axis=-1, keepdims=True)
    gc = gv - mu
    var = jnp.mean(gc * gc, axis=-1, keepdims=True)
    vs_ref[...] = gc * lax.rsqrt(var + EPS) * lng_ref[...] + lnb_ref[...]
    off += 2 * sw
    sga_ref[...] = _sigmoid(proj(off, dm)).astype(BF16)
    sgb_ref[...] = _sigmoid(proj(off + dm, dm)).astype(BF16)


def _inproj(x, gmix, win, lng, lnb, *, tile, emit_bf16_kv):
    n, dm = x.shape
    sw = lng.shape[-1]
    qw = N_HEADS * QK_DIM
    aw = N_HEADS * V_DIM
    assert win.shape == (dm, 2 * qw + aw + 2 * sw + 2 * dm)
    assert n % tile == 0
    row = lambda w: pl.BlockSpec((tile, w), lambda i: (i, 0))
    by_head = lambda w: pl.BlockSpec((tile * N_HEADS, w), lambda i: (i, 0))
    outs = [(n, qw, BF16, row(qw)),
            (n * N_HEADS, QK_DIM, F32, by_head(QK_DIM)),
            (n * N_HEADS, V_DIM, F32, by_head(V_DIM))]
    if emit_bf16_kv:
        outs += [(n, qw, BF16, row(qw)), (n, aw, BF16, row(aw))]
    outs += [(n, sw, BF16, row(sw)), (n, sw, F32, row(sw)),
             (n, dm, BF16, row(dm)), (n, dm, BF16, row(dm))]
    return pl.pallas_call(
        functools.partial(_inproj_kernel, widths=(qw, aw, sw, dm), emit_bf16_kv=emit_bf16_kv),
        grid=(n // tile,),
        in_specs=[row(dm), _const_spec((1, dm)), _const_spec(win.shape),
                  _const_spec((1, sw)), _const_spec((1, sw))],
        out_specs=[o[3] for o in outs],
        out_shape=[jax.ShapeDtypeStruct((o[0], o[1]), o[2]) for o in outs],
        compiler_params=pltpu.CompilerParams(
            dimension_semantics=("arbitrary",), vmem_limit_bytes=VMEM_LIMIT_BYTES),
        name="inproj",
    )(x, gmix, win, lng, lnb)


N_DECODE_IN = 7


def _decode_chunk_steps(pt_ref, q_ref, kn_ref, vn_ref, lamp_ref, sg_ref, ck_hbm, cv_hbm,
                        o_ref, kbuf, vbuf, sem, *, step, n_steps, n_pages):
    rps, n_rows, _ = q_ref.shape
    chunk_rows = kbuf.shape[1]
    page_rows = chunk_rows // PAGES_PER_CHUNK
    n_chunks = n_pages // PAGES_PER_CHUNK
    per_step = rps * n_chunks
    assert per_step % KV_SLOTS == 0

    def page_copies(r, c, slot, p):
        idx = pt_ref[r * n_pages + c * PAGES_PER_CHUNK + p]
        src = pl.ds(pl.multiple_of(idx * page_rows, page_rows), page_rows)
        dst = pl.ds(p * page_rows, page_rows)
        return (pltpu.make_async_copy(ck_hbm.at[src], kbuf.at[slot, dst], sem.at[slot, 0]),
                pltpu.make_async_copy(cv_hbm.at[src], vbuf.at[slot, dst], sem.at[slot, 1]))

    def start_chunk(r, c, slot):
        for p in range(PAGES_PER_CHUNK):
            for priority, cp in enumerate(page_copies(r, c, slot, p)):
                cp.start(priority=priority)

    def wait_chunk(r, c, slot):
        for p in range(PAGES_PER_CHUNK):
            for cp in page_copies(r, c, slot, p):
                cp.wait()

    lam = _lam(lamp_ref)
    gain = sg_ref[...] * (1.0 - LAM_INIT)
    own_map = (lax.broadcasted_iota(jnp.int32, (n_rows, QK_DIM), 1) // HEAD_DIM
               == lax.broadcasted_iota(jnp.int32, (n_rows, QK_DIM), 0) % 2)
    own_head = (lax.broadcasted_iota(jnp.int32, (n_rows, chunk_rows), 1) % N_HEADS
                == lax.broadcasted_iota(jnp.int32, (n_rows, chunk_rows), 0) // 2)

    def fetch(g):
        nxt, g_local = divmod(g, per_step)
        j, c = divmod(g_local, n_chunks)
        if nxt == 0:
            start_chunk(step * rps + j, c, g_local % KV_SLOTS)
        else:
            @pl.when(step + nxt < n_steps)
            def _():
                start_chunk((step + nxt) * rps + j, c, g_local % KV_SLOTS)

    assert KV_SLOTS <= per_step

    @pl.when(step == 0)
    def _():
        for g in range(KV_SLOTS):
            fetch(g)

    state = {}

    def chunk_step(g):
        j, c = divmod(g, n_chunks)
        slot = g % KV_SLOTS
        r = step * rps + j
        if c == 0:
            qb = jnp.where(own_map, q_ref[j], 0.0).astype(BF16)
            kn = kn_ref[j].astype(BF16).astype(F32)
            m = jnp.sum(qb.astype(F32) * kn, axis=-1, keepdims=True)
            state.update(qb=qb, m=m, l=jnp.ones_like(m),
                         acc=vn_ref[j].astype(BF16).astype(F32))
        wait_chunk(r, c, slot)
        kb = kbuf[slot].astype(BF16)
        s = lax.dot_general(state["qb"], kb, (((1,), (1,)), ((), ())), preferred_element_type=F32)
        s = jnp.where(own_head, s, -jnp.inf)
        m_new = jnp.maximum(state["m"], jnp.max(s, axis=-1, keepdims=True))
        alpha = jnp.exp(state["m"] - m_new)
        p = jnp.exp(s - m_new)
        state["l"] = alpha * state["l"] + jnp.sum(p, axis=-1, keepdims=True)
        state["acc"] = alpha * state["acc"] + jnp.dot(
            p.astype(BF16), vbuf[slot].astype(BF16), preferred_element_type=F32)
        state["m"] = m_new
        fetch(g + KV_SLOTS)
        if c == n_chunks - 1:
            a = state["acc"] * (1.0 / state["l"])
            for h in range(N_HEADS):
                o = a[2 * h:2 * h + 1] - lam * a[2 * h + 1:2 * h + 2]
                o_ref[j, h:h + 1, :] = _subln(o, gain)

    return [functools.partial(chunk_step, g) for g in range(per_step)]


def _run_interleaved(chunk_steps, pieces, weights):
    total, n, k, start = sum(weights), len(chunk_steps), 0, 0
    for w in weights:
        while k < n and start * n >= k * total:
            chunk_steps[k]()
            k += 1
        next(pieces, None)
        start += w
    for chunk_step in chunk_steps[k:]:
        chunk_step()
    for _ in pieces:
        raise AssertionError("more pieces than weights")


def _decode_hosting(q, k_new, v_new, cache_k, cache_v, page_table, lamp, subln_g,
                    *, n_steps, step_of, page_rows):
    n_req, n_pages = page_table.shape
    assert n_req % n_steps == 0 and n_pages % PAGES_PER_CHUNK == 0
    rps = n_req // n_steps
    per_map = lambda a: jnp.repeat(a.astype(F32).reshape(n_req, N_HEADS, QK_DIM), 2, axis=1)
    req = lambda rows, w: pl.BlockSpec((rps, rows, w), lambda *a: (step_of(*a[:-1]), 0, 0))
    hbm = pl.BlockSpec(memory_space=pl.ANY)
    chunk_rows = PAGES_PER_CHUNK * page_rows
    return dict(
        prefetch=page_table.reshape(-1),
        in_specs=[req(N_SCORE_ROWS, QK_DIM), req(N_SCORE_ROWS, QK_DIM), req(N_SCORE_ROWS, V_DIM),
                  _const_spec(lamp.shape), _const_spec(subln_g.shape), hbm, hbm],
        operands=[per_map(q), per_map(k_new), per_map(v_new), lamp, subln_g, cache_k, cache_v],
        out_spec=req(N_HEADS, V_DIM),
        out_shape=jax.ShapeDtypeStruct((n_req, N_HEADS, V_DIM), F32),
        scratch=[pltpu.VMEM((KV_SLOTS, chunk_rows, QK_DIM), cache_k.dtype),
                 pltpu.VMEM((KV_SLOTS, chunk_rows, V_DIM), cache_v.dtype),
                 pltpu.SemaphoreType.DMA((KV_SLOTS, 2))],
    )


def _prompt_attn_pieces(q_ref, k_ref, v_ref, lamp_ref, sg_ref, o_ref, *, tq):
    seq = q_ref.shape[1]
    lam = _lam(lamp_ref)
    gain = sg_ref[...] * (1.0 - LAM_INIT)
    lane = lax.broadcasted_iota(jnp.int32, (tq, QK_DIM), 1)
    first_map = lane < HEAD_DIM
    row = lax.broadcasted_iota(jnp.int32, (2 * tq, tq), 0)
    col = lax.broadcasted_iota(jnp.int32, (2 * tq, tq), 1)
    causal = col <= jnp.where(row >= tq, row - tq, row)
    n_blocks = seq // tq
    for i in range(n_blocks):
        kv = (i + 1) * tq
        q = q_ref[0, i * tq:(i + 1) * tq, :]
        zero = jnp.zeros_like(q)
        qq = jnp.concatenate([jnp.where(first_map, q, zero), jnp.where(first_map, zero, q)], axis=0)
        s = lax.dot_general(qq, k_ref[0, :kv, :], (((1,), (1,)), ((), ())),
                            preferred_element_type=F32)
        s_diag = jnp.where(causal, s[:, kv - tq:], -jnp.inf)
        s = s_diag if i == 0 else jnp.concatenate([s[:, :kv - tq], s_diag], axis=1)
        m = jnp.max(s, axis=-1, keepdims=True)
        p = jnp.exp(s - m)
        r = 1.0 / jnp.sum(p, axis=-1, keepdims=True)
        w = p[:tq] * r[:tq] - lam * (p[tq:] * r[tq:])
        o = jnp.dot(w.astype(BF16), v_ref[0, :kv, :], preferred_element_type=F32)
        o_ref[0, i * tq:(i + 1) * tq, :] = _subln(o, gain).astype(o_ref.dtype)
        if i + 1 < n_blocks:
            yield


N_ATTN_IN = 5


def _prompt_attn_decode_kernel(pt_ref, *refs, tq, n_pages):
    attn_in, rest = refs[:N_ATTN_IN], refs[N_ATTN_IN:]
    decode_in, rest = rest[:N_DECODE_IN], rest[N_DECODE_IN:]
    o_ref, od_ref, kbuf, vbuf, sem = rest
    step = pl.program_id(0) * pl.num_programs(1) + pl.program_id(1)
    n_steps = pl.num_programs(0) * pl.num_programs(1)
    chunk_steps = _decode_chunk_steps(pt_ref, *decode_in, od_ref, kbuf, vbuf, sem,
                                      step=step, n_steps=n_steps, n_pages=n_pages)
    n_blocks = attn_in[0].shape[1] // tq
    _run_interleaved(chunk_steps, _prompt_attn_pieces(*attn_in, o_ref, tq=tq),
                     [i + 1 for i in range(n_blocks)])


def _prompt_attn_with_decode(q, k, v, lamp, subln_g, decode_args, *, page_rows):
    b, s, _ = q.shape
    host = _decode_hosting(*decode_args, lamp, subln_g, n_steps=b * N_HEADS,
                           step_of=lambda bi, hi: bi * N_HEADS + hi, page_rows=page_rows)
    head = lambda w: pl.BlockSpec((1, s, w), lambda bi, hi, pt: (bi, 0, hi))
    n_pages = decode_args[5].shape[1]
    o, od = pl.pallas_call(
        functools.partial(_prompt_attn_decode_kernel, tq=ATTN_Q_TILE, n_pages=n_pages),
        grid_spec=pltpu.PrefetchScalarGridSpec(
            num_scalar_prefetch=1,
            grid=(b, N_HEADS),
            in_specs=[head(QK_DIM), head(QK_DIM), head(V_DIM), _const_spec(lamp.shape),
                      _const_spec(subln_g.shape)] + host["in_specs"],
            out_specs=[head(V_DIM), host["out_spec"]],
            scratch_shapes=host["scratch"],
        ),
        out_shape=[jax.ShapeDtypeStruct((b, s, N_HEADS * V_DIM), BF16), host["out_shape"]],
        compiler_params=pltpu.CompilerParams(
            dimension_semantics=("arbitrary", "arbitrary"), vmem_limit_bytes=VMEM_LIMIT_BYTES),
        name="prompt_attn_decode",
    )(host["prefetch"], q, k, v, lamp, subln_g, *host["operands"])
    return o, od.reshape(od.shape[0], N_HEADS * V_DIM)


MERGE_PIECE_WEIGHTS = (0.5, 1.0, 1.0, 1.4, 1.4, 1.4, 1.4, 1.4, 1.5)


def _merge_ffn_pieces(x_ref, ao_ref, u_ref, vs_ref, sga_ref, sgb_ref, ws_ref, bs_ref,
                      wao_ref, wso_ref, wo_ref, gffn_ref, wg_ref, wu_ref, wd_ref, gfin_ref,
                      y_ref, *, chunked):
    tile = x_ref.shape[0]
    sw = u_ref.shape[1]
    gd = sw // SG_GROUPS
    u = u_ref[...].astype(F32)
    if chunked:
        vs = vs_ref[...].astype(BF16)
        n_chunks = tile // CHUNK
        tri = (lax.broadcasted_iota(jnp.int32, (CHUNK, CHUNK), 0)
               >= lax.broadcasted_iota(jnp.int32, (CHUNK, CHUNK), 1))
        groups = []
        for g in range(SG_GROUPS):
            wg = jnp.where(tri, ws_ref[g], 0.0).astype(BF16)
            vg = jnp.concatenate([vs[c * CHUNK:(c + 1) * CHUNK, g * gd:(g + 1) * gd]
                                  for c in range(n_chunks)], axis=1)
            sg = jnp.dot(wg, vg, preferred_element_type=F32) + bs_ref[:, g:g + 1]
            groups.append(jnp.concatenate([sg[:, c * gd:(c + 1) * gd] for c in range(n_chunks)],
                                          axis=0))
        s = jnp.concatenate(groups, axis=1)
    else:
        s = vs_ref[...] * ws_ref[...] + bs_ref[...]
    so = (u * s).astype(BF16)
    yield
    attn = jnp.dot(ao_ref[...].astype(BF16), wao_ref[...], preferred_element_type=F32)
    gate = jnp.dot(so, wso_ref[...], preferred_element_type=F32)
    m = sga_ref[...].astype(F32) * attn + sgb_ref[...].astype(F32) * gate
    yield
    h = x_ref[...] + jnp.dot(m.astype(BF16), wo_ref[...], preferred_element_type=F32)
    hn = _rmsnorm(h, gffn_ref[...]).astype(BF16)
    yield
    d_ff = wg_ref.shape[1]
    step = d_ff // FFN_CHUNKS
    for c in range(FFN_CHUNKS):
        cols = slice(c * step, (c + 1) * step)
        a = jnp.dot(hn, wg_ref[:, cols], preferred_element_type=F32)
        yield
        b = jnp.dot(hn, wu_ref[:, cols], preferred_element_type=F32)
        f = (a * _sigmoid(a) * b).astype(BF16)
        yield
        h = h + jnp.dot(f, wd_ref[cols, :], preferred_element_type=F32)
        if c + 1 < FFN_CHUNKS:
            yield
    y_ref[...] = _rmsnorm(h, gfin_ref[...])


N_MERGE_IN = 16


def _merge_ffn_kernel(*refs, chunked):
    for _ in _merge_ffn_pieces(*refs, chunked=chunked):
        pass


def _merge_ffn_decode_kernel(pt_ref, *refs, n_pages):
    merge_in, rest = refs[:N_MERGE_IN], refs[N_MERGE_IN:]
    decode_in, rest = rest[:N_DECODE_IN], rest[N_DECODE_IN:]
    y_ref, o_ref, kbuf, vbuf, sem = rest
    chunk_steps = _decode_chunk_steps(pt_ref, *decode_in, o_ref, kbuf, vbuf, sem,
                                      step=pl.program_id(0), n_steps=pl.num_programs(0),
                                      n_pages=n_pages)
    assert len(MERGE_PIECE_WEIGHTS) == 3 + 3 * FFN_CHUNKS
    _run_interleaved(chunk_steps, _merge_ffn_pieces(*merge_in, y_ref, chunked=True),
                     MERGE_PIECE_WEIGHTS)


def _merge_ffn(x, ao, u, vs, sga, sgb, ws, bs, wao, wso, wo, gffn, wg, wu, wd, gfin, *, tile):
    n, dm = x.shape
    assert n % tile == 0 and wg.shape[1] % (FFN_CHUNKS * LANES) == 0
    row = lambda a: pl.BlockSpec((tile, a.shape[1]), lambda i: (i, 0))
    consts = (ws, bs, wao, wso, wo, gffn, wg, wu, wd, gfin)
    return pl.pallas_call(
        functools.partial(_merge_ffn_kernel, chunked=False),
        grid=(n // tile,),
        in_specs=[row(a) for a in (x, ao, u, vs, sga, sgb)] + [_const_spec(a.shape) for a in consts],
        out_specs=pl.BlockSpec((tile, dm), lambda i: (i, 0)),
        out_shape=jax.ShapeDtypeStruct((n, dm), F32),
        compiler_params=pltpu.CompilerParams(
            dimension_semantics=("arbitrary",), vmem_limit_bytes=VMEM_LIMIT_BYTES),
        name="merge_ffn_rows",
    )(x, ao, u, vs, sga, sgb, *consts)


def _merge_ffn_with_decode(x, ao, u, vs, sga, sgb, ws, bs, wao, wso, wo, gffn, wg, wu, wd, gfin,
                           lamp, subln_g, decode_args, *, tile, page_rows):
    n, dm = x.shape
    n_steps = n // tile
    assert n % tile == 0 and tile % CHUNK == 0 and wg.shape[1] % (FFN_CHUNKS * LANES) == 0
    host = _decode_hosting(*decode_args, lamp, subln_g, n_steps=n_steps, step_of=lambda i: i,
                           page_rows=page_rows)
    row = lambda a: pl.BlockSpec((tile, a.shape[1]), lambda i, pt: (i, 0))
    consts = (ws, bs, wao, wso, wo, gffn, wg, wu, wd, gfin)
    n_pages = decode_args[5].shape[1]
    y, o = pl.pallas_call(
        functools.partial(_merge_ffn_decode_kernel, n_pages=n_pages),
        grid_spec=pltpu.PrefetchScalarGridSpec(
            num_scalar_prefetch=1,
            grid=(n_steps,),
            in_specs=([row(a) for a in (x, ao, u, vs, sga, sgb)]
                      + [_const_spec(a.shape) for a in consts] + host["in_specs"]),
            out_specs=[pl.BlockSpec((tile, dm), lambda i, pt: (i, 0)), host["out_spec"]],
            scratch_shapes=host["scratch"],
        ),
        out_shape=[jax.ShapeDtypeStruct((n, dm), F32), host["out_shape"]],
        compiler_params=pltpu.CompilerParams(
            dimension_semantics=("arbitrary",), vmem_limit_bytes=VMEM_LIMIT_BYTES),
        name="merge_ffn_decode",
    )(host["prefetch"], x, ao, u, vs, sga, sgb, *consts, *host["operands"])
    return y, o.reshape(o.shape[0], N_HEADS * V_DIM)


def kernel(x_prompt, x_sample, cache_k, cache_v, page_table, w_in, lam_q1, lam_k1, lam_q2, lam_k2,
           subln_g, sg_ln_g, sg_ln_b, w_spatial, b_spatial, w_attn_out, w_sg_out, w_o,
           norm_mix_g, norm_ffn_g, w_gate, w_up, w_down, norm_final_g):
    assert w_in.shape[0] == 1 and cache_k.shape[0] == 1, "single-layer trunk"
    b, s, dm = x_prompt.shape
    bd, t_s, _ = x_sample.shape
    assert t_s == 1 and s % CHUNK == 0
    page = cache_k.shape[2]
    assert cache_k.shape[3:] == (N_HEADS, QK_DIM) and cache_v.shape[3:] == (N_HEADS, V_DIM)
    qw, aw = N_HEADS * QK_DIM, N_HEADS * V_DIM
    sw = sg_ln_g.shape[-1]
    gd = sw // SG_GROUPS
    page_rows = page * N_HEADS

    row1 = lambda a: a.reshape(1, -1)
    win = w_in[0].astype(BF16)
    gmix, lng, lnb = row1(norm_mix_g[0]), row1(sg_ln_g[0]), row1(sg_ln_b[0])
    lamp = jnp.stack([lam_q1[0], lam_k1[0], lam_q2[0], lam_k2[0]])
    subg = row1(subln_g[0])
    merge_w = (w_attn_out[0].astype(BF16), w_sg_out[0].astype(BF16), w_o[0].astype(BF16),
               row1(norm_ffn_g[0]), w_gate[0].astype(BF16), w_up[0].astype(BF16),
               w_down[0].astype(BF16), row1(norm_final_g))

    xp = x_prompt.reshape(b * s, dm)
    xs = x_sample.reshape(bd, dm)

    qs, ks, vs_, us, vss, sgas, sgbs = _inproj(
        xs, gmix, win, lng, lnb, tile=bd, emit_bf16_kv=False)
    qp, kp, vp, kbp, vbp, up, vsp, sgap, sgbp = _inproj(
        xp, gmix, win, lng, lnb, tile=ROW_TILE, emit_bf16_kv=True)

    ck, cv = cache_k.reshape(-1, QK_DIM), cache_v.reshape(-1, V_DIM)
    n_first = int(bd * ATTN_HOSTED_REQUEST_SHARE)
    decode_args = lambda lo, hi: (qs[lo:hi], ks[lo * N_HEADS:hi * N_HEADS],
                                  vs_[lo * N_HEADS:hi * N_HEADS], ck, cv, page_table[lo:hi])

    ao_p, ao_s0 = _prompt_attn_with_decode(
        qp.reshape(b, s, qw), kbp.reshape(b, s, qw), vbp.reshape(b, s, aw), lamp, subg,
        decode_args(0, n_first), page_rows=page_rows)
    yp, ao_s1 = _merge_ffn_with_decode(
        xp, ao_p.reshape(b * s, aw), up, vsp, sgap, sgbp,
        w_spatial[0], jnp.transpose(b_spatial[0]), *merge_w, lamp, subg,
        decode_args(n_first, bd), tile=ROW_TILE, page_rows=page_rows)
    ao_s = jnp.concatenate([ao_s0, ao_s1], axis=0)
    ys = _merge_ffn(xs, ao_s, us, vss, sgas, sgbs,
                    row1(jnp.repeat(w_spatial[0, :, 0, 0], gd)), row1(jnp.repeat(b_spatial[0, :, 0], gd)),
                    *merge_w, tile=bd)

    y_prompt = yp.reshape(b, s, dm)
    y_sample = ys.reshape(bd, 1, dm)
    k_prompt = kp.reshape(1, b, s, N_HEADS, QK_DIM)
    v_prompt = vp.reshape(1, b, s, N_HEADS, V_DIM)
    k_sample = ks.reshape(1, bd, 1, N_HEADS, QK_DIM)
    v_sample = vs_.reshape(1, bd, 1, N_HEADS, V_DIM)
    sg_v_prompt = vsp.reshape(b, s, sw)[:, s - CHUNK:, :].reshape(1, b, CHUNK, SG_GROUPS, gd)
    sg_v_sample = vss.reshape(1, bd, 1, SG_GROUPS, gd)
    return (y_prompt, y_sample, k_prompt, v_prompt, k_sample, v_sample, sg_v_prompt, sg_v_sample)
```
